```python
import math
import jax, jax.numpy as jnp
from jax import lax
import numpy as np

D_MODEL = 2048
BATCH = 16
SEQ = 2048
DEPTH = 2
DEC_BATCH = 4
DEC_SEQ = 8192
PAST_LEN = 128

N_HEADS = 8
HEAD_DIM = 128
V_DIM = 2 * HEAD_DIM
QK_WIDTH = N_HEADS * 2 * HEAD_DIM
ATTN_WIDTH = N_HEADS * V_DIM
Q_BLOCK = 128
N_FGROUPS = 8
FGROUP_DIM = 128
FNET_WIDTH = N_FGROUPS * FGROUP_DIM
IN_WIDTH = 2 * QK_WIDTH + ATTN_WIDTH + FNET_WIDTH + 2 * D_MODEL
D_FF = ((8 * D_MODEL + 3 * 256 - 1) // (3 * 256)) * 256
N_BUCKETS = 32
MAX_DISTANCE = 128
ALPHA = (2.0 * DEPTH) ** 0.25
BETA = (8.0 * DEPTH) ** -0.25
LN_EPS = 1e-5

kernel_name = "hybrid_diffattn_fnet_gated_encoder"


def _layernorm(x, g, b):
    xf = x.astype(jnp.float32)
    mu = jnp.mean(xf, axis=-1, keepdims=True)
    var = jnp.mean(jnp.square(xf - mu), axis=-1, keepdims=True)
    return ((xf - mu) * lax.rsqrt(var + LN_EPS) * g.astype(jnp.float32) + b.astype(jnp.float32)).astype(x.dtype)


def _rel_bucket(rel):
    nb = N_BUCKETS // 2
    ret = jnp.where(rel > 0, nb, 0)
    n = jnp.abs(rel)
    max_exact = nb // 2
    nf = jnp.maximum(n, 1).astype(jnp.float32)
    large = max_exact + (jnp.log(nf / max_exact) / math.log(MAX_DISTANCE / max_exact) * (nb - max_exact)).astype(jnp.int32)
    large = jnp.minimum(large, nb - 1)
    return ret + jnp.where(n < max_exact, n, large)


def _diff_attention(q, k, v, lam, rel_bias, lam_init):
    B, S = q.shape[0], q.shape[1]
    nblk = S // Q_BLOCK
    k1 = k[..., 0, :]
    k2 = k[..., 1, :]
    qb = (q * (HEAD_DIM ** -0.5)).reshape(B, nblk, Q_BLOCK, N_HEADS, 2, HEAD_DIM).transpose(1, 0, 3, 2, 4, 5)
    starts = jnp.arange(nblk, dtype=jnp.int32) * Q_BLOCK
    k_pos = jnp.arange(S, dtype=jnp.int32)
    lamf = lam.astype(jnp.float32)
    lam_full = jnp.exp(jnp.sum(lamf[0] * lamf[1])) - jnp.exp(jnp.sum(lamf[2] * lamf[3])) + lam_init

    def block(args):
        qblk, start = args
        q_pos = start + jnp.arange(Q_BLOCK, dtype=jnp.int32)
        bias = rel_bias[_rel_bucket(k_pos[None, :] - q_pos[:, None])]
        bias = bias.transpose(2, 0, 1).astype(jnp.float32)[None]
        s1 = jnp.einsum('bhqd,bshd->bhqs', qblk[..., 0, :], k1).astype(jnp.float32) + bias
        s2 = jnp.einsum('bhqd,bshd->bhqs', qblk[..., 1, :], k2).astype(jnp.float32) + bias
        a = jax.nn.softmax(s1, axis=-1) - lam_full * jax.nn.softmax(s2, axis=-1)
        return jnp.einsum('bhqs,bshd->bqhd', a.astype(v.dtype), v)

    o = lax.map(block, (qb, starts))
    return o.transpose(1, 0, 2, 3, 4).reshape(B, S, N_HEADS, V_DIM)


def _fourier_mix(f):
    B, S = f.shape[0], f.shape[1]
    fg = f.reshape(B, S, N_FGROUPS, FGROUP_DIM).astype(jnp.float32)
    y = jnp.fft.fft2(fg, axes=(1, 3), norm="ortho").real
    return y.reshape(B, S, FNET_WIDTH).astype(f.dtype)


def _layer(x, l, rel_bias, w_in, b_gate, lam, subln_g, w_br_attn, w_br_fnet, w_out,
           ln1_g, ln1_b, w_gu, w_down, ln2_g, ln2_b):
    B, S = x.shape[0], x.shape[1]
    lam_init = 0.8 - 0.6 * math.exp(-0.3 * l)
    h = x @ w_in[l]
    o0 = QK_WIDTH
    o1 = o0 + QK_WIDTH
    o2 = o1 + ATTN_WIDTH
    o3 = o2 + FNET_WIDTH
    q = h[..., :o0].reshape(B, S, N_HEADS, 2, HEAD_DIM)
    k = h[..., o0:o1].reshape(B, S, N_HEADS, 2, HEAD_DIM)
    v = h[..., o1:o2].reshape(B, S, N_HEADS, V_DIM)
    f = h[..., o2:o3]
    g = h[..., o3:] + b_gate[l]

    o = _diff_attention(q, k, v, lam[l], rel_bias, lam_init)
    of = o.astype(jnp.float32)
    of = of * lax.rsqrt(jnp.mean(jnp.square(of), axis=-1, keepdims=True) + LN_EPS)
    o = (of * subln_g[l].astype(jnp.float32) * (1.0 - lam_init)).astype(x.dtype)
    y_attn = o.reshape(B, S, ATTN_WIDTH) @ w_br_attn[l]

    y_fnet = _fourier_mix(f) @ w_br_fnet[l]

    gates = jax.nn.sigmoid(g)
    merged = gates[..., :D_MODEL] * y_attn + gates[..., D_MODEL:] * y_fnet
    x = _layernorm(ALPHA * x + merged @ w_out[l], ln1_g[l], ln1_b[l])

    gu = x @ w_gu[l]
    y_ffn = (jax.nn.silu(gu[..., :D_FF]) * gu[..., D_FF:]) @ w_down[l]
    return _layernorm(ALPHA * x + y_ffn, ln2_g[l], ln2_b[l])


def _trunk(x, rel_bias, ln_in_g, ln_in_b, w_in, b_gate, lam, subln_g, w_br_attn, w_br_fnet,
           w_out, ln1_g, ln1_b, w_gu, w_down, ln2_g, ln2_b):
    x = _layernorm(x, ln_in_g, ln_in_b)
    for l in range(DEPTH):
        x = _layer(x, l, rel_bias, w_in, b_gate, lam, subln_g, w_br_attn, w_br_fnet, w_out,
                   ln1_g, ln1_b, w_gu, w_down, ln2_g, ln2_b)
    return x


def setup_inputs(seed: int = 0) -> dict:
    key = jax.random.key(seed)
    ks = jax.random.split(key, 20)
    f32 = jnp.float32

    def nrm(k, shape, scale):
        return jax.random.normal(k, shape, f32) * scale

    return {
        "x_prompt": nrm(ks[0], (BATCH, SEQ, D_MODEL), 1.0),
        "x_sample": nrm(ks[1], (DEC_BATCH, DEC_SEQ, D_MODEL), 1.0),
        "rel_bias": nrm(ks[2], (N_BUCKETS, N_HEADS), 0.5),
        "ln_in_g": 1.0 + nrm(ks[3], (D_MODEL,), 0.02),
        "ln_in_b": nrm(ks[4], (D_MODEL,), 0.02),
        "w_in": nrm(ks[5], (DEPTH, D_MODEL, IN_WIDTH), D_MODEL ** -0.5),
        "b_gate": nrm(ks[6], (DEPTH, 2 * D_MODEL), 0.02),
        "lam": nrm(ks[7], (DEPTH, 4, HEAD_DIM), 0.1),
        "subln_g": 1.0 + nrm(ks[8], (DEPTH, V_DIM), 0.02),
        "w_br_attn": nrm(ks[9], (DEPTH, ATTN_WIDTH, D_MODEL), ATTN_WIDTH ** -0.5),
        "w_br_fnet": nrm(ks[10], (DEPTH, FNET_WIDTH, D_MODEL), FNET_WIDTH ** -0.5),
        "w_out": nrm(ks[11], (DEPTH, D_MODEL, D_MODEL), BETA * D_MODEL ** -0.5),
        "ln1_g": 1.0 + nrm(ks[12], (DEPTH, D_MODEL), 0.02),
        "ln1_b": nrm(ks[13], (DEPTH, D_MODEL), 0.02),
        "w_gu": nrm(ks[14], (DEPTH, D_MODEL, 2 * D_FF), D_MODEL ** -0.5),
        "w_down": nrm(ks[15], (DEPTH, D_FF, D_MODEL), BETA * D_FF ** -0.5),
        "ln2_g": 1.0 + nrm(ks[16], (DEPTH, D_MODEL), 0.02),
        "ln2_b": nrm(ks[17], (DEPTH, D_MODEL), 0.02),
    }


def reference(x_prompt, x_sample, rel_bias, ln_in_g, ln_in_b, w_in, b_gate, lam, subln_g,
              w_br_attn, w_br_fnet, w_out, ln1_g, ln1_b, w_gu, w_down, ln2_g, ln2_b):
    y_prompt = _trunk(x_prompt, rel_bias, ln_in_g, ln_in_b, w_in, b_gate, lam, subln_g,
                      w_br_attn, w_br_fnet, w_out, ln1_g, ln1_b, w_gu, w_down, ln2_g, ln2_b)
    y_sample = _trunk(x_sample, rel_bias, ln_in_g, ln_in_b, w_in, b_gate, lam, subln_g,
                      w_br_attn, w_br_fnet, w_out, ln1_g, ln1_b, w_gu, w_down, ln2_g, ln2_b)
    return (y_prompt, y_sample)
```

```python
import functools
import math

import jax
import jax.numpy as jnp
from jax import lax
from jax.experimental import pallas as pl
from jax.experimental.pallas import tpu as pltpu

N_HEADS = 8
HEAD_DIM = 128
V_DIM = 2 * HEAD_DIM
N_FGROUPS = 8
FGROUP_DIM = 128
N_BUCKETS = 32
MAX_DISTANCE = 128
LN_EPS = 1e-5

LANES = 128
ATTN_TILE = 256
VMEM_LIMIT_BYTES = 56 * 1024 * 1024

F32 = jnp.float32
BF16 = jnp.bfloat16


def _tile(n, pref):
    if n <= pref:
        return n
    t = (pref // LANES) * LANES
    while t >= LANES:
        if n % t == 0:
            return t
        t -= LANES
    raise ValueError(f"no lane-aligned tile for {n}")


def _params(*sem):
    return pltpu.CompilerParams(dimension_semantics=sem, vmem_limit_bytes=VMEM_LIMIT_BYTES)


def _layernorm_rows(y, g, b):
    mu = jnp.mean(y, axis=-1, keepdims=True)
    yc = y - mu
    var = jnp.mean(yc * yc, axis=-1, keepdims=True)
    return yc * lax.rsqrt(var + LN_EPS) * g + b


def _ln_kernel(x_ref, g_ref, b_ref, of_ref, ob_ref):
    y = _layernorm_rows(x_ref[...], g_ref[...], b_ref[...])
    of_ref[...] = y
    ob_ref[...] = y.astype(BF16)


def _input_layernorm(x, g, b):
    t, d = x.shape
    bm = _tile(t, 512)
    row = pl.BlockSpec((bm, d), lambda i: (i, 0))
    vec = pl.BlockSpec((1, d), lambda i: (0, 0))
    return pl.pallas_call(
        _ln_kernel,
        grid=(t // bm,),
        in_specs=[row, vec, vec],
        out_specs=[row, row],
        out_shape=[jax.ShapeDtypeStruct((t, d), F32), jax.ShapeDtypeStruct((t, d), BF16)],
        compiler_params=_params("parallel"),
        name="input_layernorm",
    )(x, g.reshape(1, d), b.reshape(1, d))


def _proj_kernel(x_ref, w_ref, s_ref, o_ref):
    acc = jnp.dot(x_ref[...], w_ref[...], preferred_element_type=F32)
    o_ref[...] = (acc * s_ref[...]).astype(o_ref.dtype)


def _gate_kernel(x_ref, w_ref, b_ref, o_ref):
    acc = jnp.dot(x_ref[...], w_ref[...], preferred_element_type=F32)
    o_ref[...] = jax.nn.sigmoid(acc + b_ref[...]).astype(o_ref.dtype)


def _in_projection(xb, w_in, layer, col_scale, b_gate, n_main):
    t, d = xb.shape
    n_gate = w_in.shape[-1] - n_main
    bm = _tile(t, 1024)

    bn = _tile(n_main, 1024)
    qkvf = pl.pallas_call(
        _proj_kernel,
        grid=(n_main // bn, t // bm),
        in_specs=[
            pl.BlockSpec((bm, d), lambda j, i: (i, 0)),
            pl.BlockSpec((None, d, bn), lambda j, i: (layer, 0, j)),
            pl.BlockSpec((1, bn), lambda j, i: (0, j)),
        ],
        out_specs=pl.BlockSpec((bm, bn), lambda j, i: (i, j)),
        out_shape=jax.ShapeDtypeStruct((t, n_main), BF16),
        compiler_params=_params("parallel", "parallel"),
        name="in_proj_qkvf",
    )(xb, w_in, col_scale)

    bg = _tile(n_gate, 1024)
    assert n_main % bg == 0
    off = n_main // bg
    gates = pl.pallas_call(
        _gate_kernel,
        grid=(n_gate // bg, t // bm),
        in_specs=[
            pl.BlockSpec((bm, d), lambda j, i: (i, 0)),
            pl.BlockSpec((None, d, bg), lambda j, i: (layer, 0, off + j)),
            pl.BlockSpec((None, 1, bg), lambda j, i: (layer, 0, j)),
        ],
        out_specs=pl.BlockSpec((bm, bg), lambda j, i: (i, j)),
        out_shape=jax.ShapeDtypeStruct((t, n_gate), BF16),
        compiler_params=_params("parallel", "parallel"),
        name="in_proj_gates",
    )(xb, w_in, b_gate)
    return qkvf, gates


def _t5_bucket(rel):
    nb = N_BUCKETS // 2
    ret = jnp.where(rel > 0, nb, 0)
    n = jnp.abs(rel)
    max_exact = nb // 2
    nf = jnp.maximum(n, 1).astype(F32)
    large = max_exact + (jnp.log(nf / max_exact) / math.log(MAX_DISTANCE / max_exact) * (nb - max_exact)).astype(jnp.int32)
    large = jnp.minimum(large, nb - 1)
    return ret + jnp.where(n < max_exact, n, large)


N_BIAS_TILES = 5


def _bias_tile_kernel(tab_ref, ids_ref, o_ref):
    h = pl.program_id(0)
    ids = ids_ref[...]
    acc = jnp.zeros(ids.shape, F32)
    for b in range(N_BUCKETS):
        acc = jnp.where(ids == b, tab_ref[b, h], acc)
    o_ref[...] = acc


def _bias_tiles(rel_bias):
    tb = ATTN_TILE
    assert tb >= MAX_DISTANCE
    i = lax.broadcasted_iota(jnp.int32, (N_BIAS_TILES, tb, tb), 1)
    j = lax.broadcasted_iota(jnp.int32, (N_BIAS_TILES, tb, tb), 2)
    d = lax.broadcasted_iota(jnp.int32, (N_BIAS_TILES, tb, tb), 0) - 2
    ids = _t5_bucket(d * tb + j - i).reshape(N_BIAS_TILES * tb, tb)
    out = pl.pallas_call(
        _bias_tile_kernel,
        grid=(N_HEADS,),
        in_specs=[
            pl.BlockSpec(memory_space=pltpu.SMEM),
            pl.BlockSpec((N_BIAS_TILES * tb, tb), lambda h: (0, 0)),
        ],
        out_specs=pl.BlockSpec((None, N_BIAS_TILES * tb, tb), lambda h: (h, 0, 0)),
        out_shape=jax.ShapeDtypeStruct((N_HEADS, N_BIAS_TILES * tb, tb), F32),
        compiler_params=_params("parallel"),
        name="t5_bias_tiles",
    )(rel_bias, ids)
    return out.reshape(N_HEADS, N_BIAS_TILES, tb, tb)


def _attn_kernel(lam_ref, g_ref, q_ref, k_ref, v_ref, band_ref, o_ref,
                 acc1_ref, acc2_ref, m1_ref, m2_ref, l1_ref, l2_ref, *, n_kv, lam_init):
    tq = ATTN_TILE
    tk = ATTN_TILE
    qi = pl.program_id(2)
    q = q_ref[...]
    q1 = q[:, :HEAD_DIM]
    q2 = q[:, HEAD_DIM:]

    neg = jnp.full((tq, LANES), -jnp.inf, F32)
    zero = jnp.zeros((tq, LANES), F32)
    m1_ref[...] = neg
    m2_ref[...] = neg
    l1_ref[...] = zero
    l2_ref[...] = zero
    acc1_ref[...] = jnp.zeros((tq, V_DIM), F32)
    acc2_ref[...] = jnp.zeros((tq, V_DIM), F32)

    nt = (((1,), (1,)), ((), ()))
    reps = tk // LANES

    def wide(x):
        return jnp.concatenate([x] * reps, axis=1)

    def one_map(qm, km, bias, vc, acc_ref, m_ref, l_ref):
        s = lax.dot_general(qm, km, nt, preferred_element_type=F32) + bias
        m_prev = m_ref[...]
        m_new = jnp.maximum(m_prev, jnp.max(s, axis=-1, keepdims=True))
        alpha = jnp.exp(m_prev - m_new)
        p = jnp.exp(s - wide(m_new))
        l_ref[...] = alpha * l_ref[...] + jnp.sum(p, axis=-1, keepdims=True)
        m_ref[...] = m_new
        pv = jnp.dot(p.astype(BF16), vc, preferred_element_type=F32)
        acc_ref[...] = jnp.concatenate([alpha] * (V_DIM // LANES), axis=1) * acc_ref[...] + pv

    def body(c, carry):
        start = pl.multiple_of(c * tk, tk)
        kc = k_ref[pl.ds(start, tk), :]
        vc = v_ref[pl.ds(start, tk), :]
        bias = band_ref[jnp.clip(c - qi, -2, 2) + 2]
        one_map(q1, kc[:, :HEAD_DIM], bias, vc, acc1_ref, m1_ref, l1_ref)
        one_map(q2, kc[:, HEAD_DIM:], bias, vc, acc2_ref, m2_ref, l2_ref)
        return carry

    lax.fori_loop(0, n_kv, body, 0)

    lam = lam_ref[...]
    lam_full = (jnp.exp(jnp.sum(lam[0:1] * lam[1:2], axis=-1, keepdims=True))
                - jnp.exp(jnp.sum(lam[2:3] * lam[3:4], axis=-1, keepdims=True)) + lam_init)
    vrep = V_DIM // LANES
    o1 = acc1_ref[...] / jnp.concatenate([l1_ref[...]] * vrep, axis=1)
    o2 = acc2_ref[...] / jnp.concatenate([l2_ref[...]] * vrep, axis=1)
    of = o1 - lam_full * o2
    of = of * lax.rsqrt(jnp.mean(of * of, axis=-1, keepdims=True) + LN_EPS)
    o_ref[...] = (of * g_ref[...] * (1.0 - lam_init)).astype(o_ref.dtype)


def _diff_attention(qkvf, lam_l, subln_g_l, band, batch, seq, lam_init):
    tq = ATTN_TILE
    assert seq % tq == 0 and seq // tq >= 1
    n_q = seq // tq
    qkv3 = qkvf.reshape(batch, seq, qkvf.shape[-1])
    kern = functools.partial(_attn_kernel, n_kv=n_q, lam_init=lam_init)
    out = pl.pallas_call(
        kern,
        grid=(batch, N_HEADS, n_q),
        in_specs=[
            pl.BlockSpec((4, HEAD_DIM), lambda b, h, i: (0, 0)),
            pl.BlockSpec((1, V_DIM), lambda b, h, i: (0, 0)),
            pl.BlockSpec((None, tq, V_DIM), lambda b, h, i: (b, i, h)),
            pl.BlockSpec((None, seq, V_DIM), lambda b, h, i: (b, 0, N_HEADS + h)),
            pl.BlockSpec((None, seq, V_DIM), lambda b, h, i: (b, 0, 2 * N_HEADS + h)),
            pl.BlockSpec((None, N_BIAS_TILES, tq, tq), lambda b, h, i: (h, 0, 0, 0)),
        ],
        out_specs=pl.BlockSpec((None, tq, V_DIM), lambda b, h, i: (b, i, h)),
        out_shape=jax.ShapeDtypeStruct((batch, seq, N_HEADS * V_DIM), BF16),
        scratch_shapes=[
            pltpu.VMEM((tq, V_DIM), F32), pltpu.VMEM((tq, V_DIM), F32),
            pltpu.VMEM((tq, LANES), F32), pltpu.VMEM((tq, LANES), F32),
            pltpu.VMEM((tq, LANES), F32), pltpu.VMEM((tq, LANES), F32),
        ],
        compiler_params=_params("parallel", "parallel", "arbitrary"),
        name="diff_attention",
    )(lam_l, subln_g_l.reshape(1, V_DIM), qkv3, qkv3, qkv3, band)
    return out.reshape(batch * seq, N_HEADS * V_DIM)


def _dft_tables(n, scale):
    j = lax.broadcasted_iota(jnp.int32, (n, n), 0)
    k = lax.broadcasted_iota(jnp.int32, (n, n), 1)
    ang = ((j * k) % n).astype(F32) * (2.0 * math.pi / n)
    return jnp.cos(ang) * scale, jnp.sin(ang) * scale


def _chan_dft_kernel(f_ref, cs_ref, o_ref):
    cs = cs_ref[...]
    for g in range(N_FGROUPS):
        lo = g * FGROUP_DIM
        z = jnp.dot(f_ref[:, lo:lo + FGROUP_DIM], cs, preferred_element_type=F32)
        o_ref[0, :, lo:lo + FGROUP_DIM] = z[:, :FGROUP_DIM].astype(o_ref.dtype)
        o_ref[1, :, lo:lo + FGROUP_DIM] = z[:, FGROUP_DIM:].astype(o_ref.dtype)


def _seq_dft_kernel(w_ref, z_ref, o_ref, acc_ref):
    k = pl.program_id(3)

    @pl.when(k == 0)
    def _():
        acc_ref[...] = jnp.zeros_like(acc_ref)

    acc_ref[...] += jnp.dot(w_ref[...], z_ref[...], preferred_element_type=F32)

    @pl.when(k == pl.num_programs(3) - 1)
    def _():
        o_ref[...] = acc_ref[...].astype(o_ref.dtype)


def _fourier_mix(qkvf, f_col_block, chan_cs, seq_w, batch, seq):
    fw = N_FGROUPS * FGROUP_DIM
    bm = _tile(seq, 1024)
    qkv3 = qkvf.reshape(batch, seq, qkvf.shape[-1])
    z = pl.pallas_call(
        _chan_dft_kernel,
        grid=(batch, seq // bm),
        in_specs=[
            pl.BlockSpec((None, bm, fw), lambda b, i: (b, i, f_col_block)),
            pl.BlockSpec((FGROUP_DIM, 2 * FGROUP_DIM), lambda b, i: (0, 0)),
        ],
        out_specs=pl.BlockSpec((None, 2, bm, fw), lambda b, i: (b, 0, i, 0)),
        out_shape=jax.ShapeDtypeStruct((batch, 2, seq, fw), BF16),
        compiler_params=_params("parallel", "parallel"),
        name="fnet_channel_dft",
    )(qkv3, chan_cs)
    z = z.reshape(batch, 2 * seq, fw)

    bn = _tile(fw, 1024)
    bk = _tile(2 * seq, 2048)
    y = pl.pallas_call(
        _seq_dft_kernel,
        grid=(batch, seq // bm, fw // bn, (2 * seq) // bk),
        in_specs=[
            pl.BlockSpec((bm, bk), lambda b, i, j, k: (i, k)),
            pl.BlockSpec((None, bk, bn), lambda b, i, j, k: (b, k, j)),
        ],
        out_specs=pl.BlockSpec((None, bm, bn), lambda b, i, j, k: (b, i, j)),
        out_shape=jax.ShapeDtypeStruct((batch, seq, fw), BF16),
        scratch_shapes=[pltpu.VMEM((bm, bn), F32)],
        compiler_params=_params("parallel", "parallel", "parallel", "arbitrary"),
        name="fnet_sequence_dft",
    )(seq_w, z)
    return y.reshape(batch * seq, fw)


def _merge_kernel(o_ref, f_ref, wa_ref, wf_ref, g1_ref, g2_ref, out_ref):
    ya = jnp.dot(o_ref[...], wa_ref[...], preferred_element_type=F32)
    yf = jnp.dot(f_ref[...], wf_ref[...], preferred_element_type=F32)
    out_ref[...] = (g1_ref[...].astype(F32) * ya + g2_ref[...].astype(F32) * yf).astype(out_ref.dtype)


def _gated_merge(o, fm, gates, w_attn, w_fnet, layer):
    t, ka = o.shape
    kf = fm.shape[1]
    d = w_attn.shape[-1]
    bm = _tile(t, 1024)
    bn = _tile(d, 512)
    nb = d // bn
    return pl.pallas_call(
        _merge_kernel,
        grid=(nb, t // bm),
        in_specs=[
            pl.BlockSpec((bm, ka), lambda j, i: (i, 0)),
            pl.BlockSpec((bm, kf), lambda j, i: (i, 0)),
            pl.BlockSpec((None, ka, bn), lambda j, i: (layer, 0, j)),
            pl.BlockSpec((None, kf, bn), lambda j, i: (layer, 0, j)),
            pl.BlockSpec((bm, bn), lambda j, i: (i, j)),
            pl.BlockSpec((bm, bn), lambda j, i: (i, nb + j)),
        ],
        out_specs=pl.BlockSpec((bm, bn), lambda j, i: (i, j)),
        out_shape=jax.ShapeDtypeStruct((t, d), BF16),
        compiler_params=_params("parallel", "parallel"),
        name="gated_merge",
    )(o, fm, w_attn, w_fnet, gates, gates)


def _proj_res_ln_kernel(a_ref, w_ref, x_ref, g_ref, b_ref, of_ref, ob_ref, acc_ref, *, alpha):
    k = pl.program_id(1)

    @pl.when(k == 0)
    def _():
        acc_ref[...] = jnp.zeros_like(acc_ref)

    acc_ref[...] += jnp.dot(a_ref[...], w_ref[...], preferred_element_type=F32)

    @pl.when(k == pl.num_programs(1) - 1)
    def _():
        y = _layernorm_rows(alpha * x_ref[...] + acc_ref[...], g_ref[...], b_ref[...])
        of_ref[...] = y
        ob_ref[...] = y.astype(BF16)


def _proj_residual_layernorm(a, w, layer, xf, g, b, alpha, bk_pref):
    t, kdim = a.shape
    d = w.shape[-1]
    bm = _tile(t, 512)
    bk = _tile(kdim, bk_pref)
    row = pl.BlockSpec((bm, d), lambda i, k: (i, 0))
    vec = pl.BlockSpec((None, 1, d), lambda i, k: (layer, 0, 0))
    return pl.pallas_call(
        functools.partial(_proj_res_ln_kernel, alpha=alpha),
        grid=(t // bm, kdim // bk),
        in_specs=[
            pl.BlockSpec((bm, bk), lambda i, k: (i, k)),
            pl.BlockSpec((None, bk, d), lambda i, k: (layer, k, 0)),
            row, vec, vec,
        ],
        out_specs=[row, row],
        out_shape=[jax.ShapeDtypeStruct((t, d), F32), jax.ShapeDtypeStruct((t, d), BF16)],
        scratch_shapes=[pltpu.VMEM((bm, d), F32)],
        compiler_params=_params("parallel", "arbitrary"),
        name="proj_residual_layernorm",
    )(a, w, xf, g, b)


def _swiglu_kernel(x_ref, wg_ref, wu_ref, o_ref):
    x = x_ref[...]
    gate = jnp.dot(x, wg_ref[...], preferred_element_type=F32)
    up = jnp.dot(x, wu_ref[...], preferred_element_type=F32)
    o_ref[...] = (gate * jax.nn.sigmoid(gate) * up).astype(o_ref.dtype)


def _swiglu_up(xb, w_gu, layer, d_ff):
    t, d = xb.shape
    bm = _tile(t, 1024)
    bn = _tile(d_ff, 512)
    nb = d_ff // bn
    return pl.pallas_call(
        _swiglu_kernel,
        grid=(nb, t // bm),
        in_specs=[
            pl.BlockSpec((bm, d), lambda j, i: (i, 0)),
            pl.BlockSpec((None, d, bn), lambda j, i: (layer, 0, j)),
            pl.BlockSpec((None, d, bn), lambda j, i: (layer, 0, nb + j)),
        ],
        out_specs=pl.BlockSpec((bm, bn), lambda j, i: (i, j)),
        out_shape=jax.ShapeDtypeStruct((t, d_ff), BF16),
        compiler_params=_params("parallel", "parallel"),
        name="swiglu_up",
    )(xb, w_gu, w_gu)


def _trunk(x, band, chan_cs, seq_w, p):
    batch, seq, d = x.shape
    depth = p["w_in"].shape[0]
    alpha = (2.0 * depth) ** 0.25
    qk_width = N_HEADS * 2 * HEAD_DIM
    attn_width = N_HEADS * V_DIM
    fnet_width = N_FGROUPS * FGROUP_DIM
    n_main = 2 * qk_width + attn_width + fnet_width
    assert (2 * qk_width + attn_width) % fnet_width == 0
    f_col_block = (2 * qk_width + attn_width) // fnet_width
    d_ff = p["w_down"].shape[1]

    col_scale = jnp.concatenate(
        [jnp.full((qk_width,), HEAD_DIM ** -0.5, F32), jnp.ones((n_main - qk_width,), F32)]).reshape(1, n_main)

    xf, xb = _input_layernorm(x.reshape(batch * seq, d), p["ln_in_g"], p["ln_in_b"])
    for l in range(depth):
        lam_init = 0.8 - 0.6 * math.exp(-0.3 * l)
        qkvf, gates = _in_projection(xb, p["w_in"], l, col_scale, p["b_gate"], n_main)
        o = _diff_attention(qkvf, p["lam"][l], p["subln_g"][l], band, batch, seq, lam_init)
        fm = _fourier_mix(qkvf, f_col_block, chan_cs, seq_w, batch, seq)
        merged = _gated_merge(o, fm, gates, p["w_br_attn"], p["w_br_fnet"], l)
        xf, xb = _proj_residual_layernorm(merged, p["w_out"], l, xf, p["ln1_g"], p["ln1_b"], alpha, 2048)
        act = _swiglu_up(xb, p["w_gu"], l, d_ff)
        xf, xb = _proj_residual_layernorm(act, p["w_down"], l, xf, p["ln2_g"], p["ln2_b"], alpha, 1408)
    return xf.reshape(batch, seq, d)


def kernel(x_prompt, x_sample, rel_bias, ln_in_g, ln_in_b, w_in, b_gate, lam, subln_g, w_br_attn, w_br_fnet,
           w_out, ln1_g, ln1_b, w_gu, w_down, ln2_g, ln2_b):
    depth, d = ln1_g.shape
    p = {
        "ln_in_g": ln_in_g, "ln_in_b": ln_in_b,
        "w_in": w_in.astype(BF16), "b_gate": b_gate.reshape(depth, 1, -1),
        "lam": lam, "subln_g": subln_g,
        "w_br_attn": w_br_attn.astype(BF16), "w_br_fnet": w_br_fnet.astype(BF16),
        "w_out": w_out.astype(BF16), "w_gu": w_gu.astype(BF16), "w_down": w_down.astype(BF16),
        "ln1_g": ln1_g.reshape(depth, 1, d), "ln1_b": ln1_b.reshape(depth, 1, d),
        "ln2_g": ln2_g.reshape(depth, 1, d), "ln2_b": ln2_b.reshape(depth, 1, d),
    }
    band = _bias_tiles(rel_bias)
    cc, sc = _dft_tables(FGROUP_DIM, FGROUP_DIM ** -0.5)
    chan_cs = jnp.concatenate([cc, sc], axis=1).astype(BF16)

    outs = []
    for x in (x_prompt, x_sample):
        seq = x.shape[1]
        cs, ss = _dft_tables(seq, seq ** -0.5)
        seq_w = jnp.concatenate([cs, -ss], axis=1).astype(BF16)
        outs.append(_trunk(x, band, chan_cs, seq_w, p))
    return tuple(outs)
```

```python
import functools
import math

import jax
import jax.numpy as jnp
from jax import lax
from jax.experimental import pallas as pl
from jax.experimental.pallas import tpu as pltpu

N_HEADS = 8
HEAD_DIM = 128
V_DIM = 2 * HEAD_DIM
N_FGROUPS = 8
FGROUP_DIM = 128
N_BUCKETS = 32
MAX_DISTANCE = 128
LN_EPS = 1e-5

LANES = 128
ATTN_TILE = 256
ATTN_KV_CHUNK = 2048
ATTN_ROWS_PER_STEP = 2048
VMEM_LIMIT_BYTES = 56 * 1024 * 1024

F32 = jnp.float32
BF16 = jnp.bfloat16


def _tile(n, pref):
    if n <= pref:
        return n
    t = (pref // LANES) * LANES
    while t >= LANES:
        if n % t == 0:
            return t
        t -= LANES
    raise ValueError(f"no lane-aligned tile for {n}")


def _params(*sem, flags=None):
    return pltpu.CompilerParams(dimension_semantics=sem, vmem_limit_bytes=VMEM_LIMIT_BYTES, flags=flags)


def _layernorm_rows(y, g, b):
    mu = jnp.mean(y, axis=-1, keepdims=True)
    yc = y - mu
    var = jnp.mean(yc * yc, axis=-1, keepdims=True)
    return yc * lax.rsqrt(var + LN_EPS) * g + b


def _ln_kernel(x_ref, g_ref, b_ref, of_ref, ob_ref):
    y = _layernorm_rows(x_ref[...], g_ref[...], b_ref[...])
    of_ref[...] = y
    ob_ref[...] = y.astype(BF16)


def _input_layernorm(x, g, b):
    t, d = x.shape
    bm = _tile(t, 512)
    row = pl.BlockSpec((bm, d), lambda i: (i, 0))
    vec = pl.BlockSpec((1, d), lambda i: (0, 0))
    return pl.pallas_call(
        _ln_kernel,
        grid=(t // bm,),
        in_specs=[row, vec, vec],
        out_specs=[row, row],
        out_shape=[jax.ShapeDtypeStruct((t, d), F32), jax.ShapeDtypeStruct((t, d), BF16)],
        compiler_params=_params("parallel"),
        name="input_layernorm",
    )(x, g.reshape(1, d), b.reshape(1, d))


def _proj_kernel(x_ref, w_ref, s_ref, o_ref):
    acc = jnp.dot(x_ref[...], w_ref[...], preferred_element_type=F32)
    o_ref[...] = (acc * s_ref[...]).astype(o_ref.dtype)


def _gate_kernel(x_ref, w_ref, b_ref, o_ref):
    acc = jnp.dot(x_ref[...], w_ref[...], preferred_element_type=F32)
    o_ref[...] = jax.nn.sigmoid(acc + b_ref[...]).astype(o_ref.dtype)


def _in_projection(xb, w_in, layer, col_scale, b_gate, n_main):
    t, d = xb.shape
    n_gate = w_in.shape[-1] - n_main
    bm = _tile(t, 1024)

    bn = _tile(n_main, 1024)
    qkvf = pl.pallas_call(
        _proj_kernel,
        grid=(n_main // bn, t // bm),
        in_specs=[
            pl.BlockSpec((bm, d), lambda j, i: (i, 0)),
            pl.BlockSpec((None, d, bn), lambda j, i: (layer, 0, j)),
            pl.BlockSpec((1, bn), lambda j, i: (0, j)),
        ],
        out_specs=pl.BlockSpec((bm, bn), lambda j, i: (i, j)),
        out_shape=jax.ShapeDtypeStruct((t, n_main), BF16),
        compiler_params=_params("parallel", "parallel"),
        name="in_proj_qkvf",
    )(xb, w_in, col_scale)

    bg = _tile(n_gate, 1024)
    assert n_main % bg == 0
    off = n_main // bg
    gates = pl.pallas_call(
        _gate_kernel,
        grid=(n_gate // bg, t // bm),
        in_specs=[
            pl.BlockSpec((bm, d), lambda j, i: (i, 0)),
            pl.BlockSpec((None, d, bg), lambda j, i: (layer, 0, off + j)),
            pl.BlockSpec((None, 1, bg), lambda j, i: (layer, 0, j)),
        ],
        out_specs=pl.BlockSpec((bm, bg), lambda j, i: (i, j)),
        out_shape=jax.ShapeDtypeStruct((t, n_gate), BF16),
        compiler_params=_params("parallel", "parallel"),
        name="in_proj_gates",
    )(xb, w_in, b_gate)
    return qkvf, gates


def _t5_bucket(rel):
    nb = N_BUCKETS // 2
    ret = jnp.where(rel > 0, nb, 0)
    n = jnp.abs(rel)
    max_exact = nb // 2
    nf = jnp.maximum(n, 1).astype(F32)
    large = max_exact + (jnp.log(nf / max_exact) / math.log(MAX_DISTANCE / max_exact) * (nb - max_exact)).astype(jnp.int32)
    large = jnp.minimum(large, nb - 1)
    return ret + jnp.where(n < max_exact, n, large)


N_BIAS_TILES = 5


def _bias_tile_kernel(tab_ref, ids_ref, o_ref):
    h = pl.program_id(0)
    ids = ids_ref[...]
    acc = jnp.zeros(ids.shape, F32)
    for b in range(N_BUCKETS):
        acc = jnp.where(ids == b, tab_ref[b, h], acc)
    o_ref[...] = acc


def _bias_tiles(rel_bias):
    tb = ATTN_TILE
    assert tb >= MAX_DISTANCE
    i = lax.broadcasted_iota(jnp.int32, (N_BIAS_TILES, tb, tb), 1)
    j = lax.broadcasted_iota(jnp.int32, (N_BIAS_TILES, tb, tb), 2)
    d = lax.broadcasted_iota(jnp.int32, (N_BIAS_TILES, tb, tb), 0) - 2
    ids = _t5_bucket(d * tb + j - i).reshape(N_BIAS_TILES * tb, tb)
    out = pl.pallas_call(
        _bias_tile_kernel,
        grid=(N_HEADS,),
        in_specs=[
            pl.BlockSpec(memory_space=pltpu.SMEM),
            pl.BlockSpec((N_BIAS_TILES * tb, tb), lambda h: (0, 0)),
        ],
        out_specs=pl.BlockSpec((None, N_BIAS_TILES * tb, tb), lambda h: (h, 0, 0)),
        out_shape=jax.ShapeDtypeStruct((N_HEADS, N_BIAS_TILES * tb, tb), F32),
        compiler_params=_params("parallel"),
        name="t5_bias_tiles",
    )(rel_bias, ids)
    return out.reshape(N_HEADS, N_BIAS_TILES, tb, tb)


def _lane_wide(x, width):
    return jnp.concatenate([x] * (width // LANES), axis=1)


def _attn_kernel(lam_init_ref, lam_ref, g_ref, q_ref, k_ref, v_ref, band_ref, o_ref,
                 s_even_ref, s_odd_ref, mx_even_ref, mx_odd_ref, m_ref, l_ref, acc_ref, *, n_blocks, n_chunks, tk):
    tq = ATTN_TILE
    sub = tk // ATTN_TILE
    n_items = n_blocks * n_chunks
    assert n_items % 2 == 0
    group = pl.program_id(2)
    nt = (((1,), (1,)), ((), ()))
    even = (s_even_ref, mx_even_ref)
    odd = (s_odd_ref, mx_odd_ref)

    lam_init = lam_init_ref[0]
    lam = lam_ref[...]
    lam_full = (jnp.exp(jnp.sum(lam[0:1] * lam[1:2], axis=-1, keepdims=True))
                - jnp.exp(jnp.sum(lam[2:3] * lam[3:4], axis=-1, keepdims=True)) + lam_init)
    gain = g_ref[...] * (1.0 - lam_init)

    if n_chunks > 1:
        m_ref[...] = jnp.zeros(m_ref.shape, F32)
        l_ref[...] = jnp.zeros(l_ref.shape, F32)
        acc_ref[...] = jnp.zeros(acc_ref.shape, F32)

    def position(t):
        if isinstance(t, int):
            return t // n_chunks, t % n_chunks
        return lax.div(t, n_chunks), lax.rem(t, n_chunks)

    def rows_of(i):
        return pl.ds(i * tq, tq) if isinstance(i, int) else pl.ds(pl.multiple_of(i * tq, tq), tq)

    def keys_of(c, u):
        start = c * tk + u * ATTN_TILE
        return pl.ds(start if isinstance(c, int) else pl.multiple_of(start, ATTN_TILE), ATTN_TILE)

    def half_max(x):
        return jnp.maximum(x[:, :LANES], x[:, LANES:])

    def half_sum(x):
        return x[:, :LANES] + x[:, LANES:]

    def finish_rows(i, o1, o2):
        of = o1 - lam_full * o2
        of = of * lax.rsqrt(jnp.mean(of * of, axis=-1, keepdims=True) + LN_EPS)
        o_ref[rows_of(i), :] = (of * gain).astype(o_ref.dtype)

    def run(a_item, a_bufs, b_item, b_bufs):
        if a_item is not None:
            a_i, a_c = position(a_item)
            a_s_ref, a_mx_ref = a_bufs
            a_row = group * n_blocks + a_i
            a_q = [q_ref[rows_of(a_i), mi * HEAD_DIM:(mi + 1) * HEAD_DIM] for mi in range(2)]
            a_max = [None, None]
        if b_item is not None:
            b_i, b_c = position(b_item)
            b_s_ref, b_mx_ref = b_bufs
            if n_chunks == 1:
                b_m = [b_mx_ref[mi] for mi in range(2)]
            else:
                b_m_prev = [jnp.where(b_c == 0, -jnp.inf, m_ref[mi]) for mi in range(2)]
                b_m = [jnp.maximum(b_m_prev[mi], b_mx_ref[mi]) for mi in range(2)]
                b_alpha = [jnp.exp(b_m_prev[mi] - b_m[mi]) for mi in range(2)]
            b_m_wide = [_lane_wide(b_m[mi], ATTN_TILE) for mi in range(2)]
            b_sum = [None, None]
            b_acc = [None, None]

        for u in range(sub):
            cols = slice(u * ATTN_TILE, (u + 1) * ATTN_TILE)
            if a_item is not None:
                tile = band_ref[jnp.clip(a_c * sub + u - a_row, -2, 2) + 2]
                for mi in range(2):
                    s = lax.dot_general(a_q[mi], k_ref[keys_of(a_c, u), mi * HEAD_DIM:(mi + 1) * HEAD_DIM],
                                        nt, preferred_element_type=F32) + tile
                    a_s_ref[mi, :, cols] = s
                    a_max[mi] = half_max(s) if a_max[mi] is None else jnp.maximum(a_max[mi], half_max(s))
            if b_item is not None:
                vu = v_ref[keys_of(b_c, u), :]
                for mi in range(2):
                    p = jnp.exp(b_s_ref[mi, :, cols] - b_m_wide[mi])
                    b_sum[mi] = half_sum(p) if b_sum[mi] is None else b_sum[mi] + half_sum(p)
                    pv = jnp.dot(p.astype(BF16), vu, preferred_element_type=F32)
                    b_acc[mi] = pv if b_acc[mi] is None else b_acc[mi] + pv

        if a_item is not None:
            for mi in range(2):
                a_mx_ref[mi] = jnp.broadcast_to(jnp.max(a_max[mi], axis=-1, keepdims=True), (tq, LANES))
        if b_item is not None:
            b_l = [jnp.sum(b_sum[mi], axis=-1, keepdims=True) for mi in range(2)]
            if n_chunks == 1:
                finish_rows(b_i, b_acc[0] / b_l[0], b_acc[1] / b_l[1])
            else:
                for mi in range(2):
                    l_ref[mi] = b_alpha[mi] * l_ref[mi] + b_l[mi]
                    m_ref[mi] = b_m[mi]
                    acc_ref[mi] = _lane_wide(b_alpha[mi], V_DIM) * acc_ref[mi] + b_acc[mi]

                def last_chunk():
                    finish_rows(b_i, acc_ref[0] / _lane_wide(l_ref[0], V_DIM),
                                acc_ref[1] / _lane_wide(l_ref[1], V_DIM))
                if isinstance(b_c, int):
                    if b_c == n_chunks - 1:
                        last_chunk()
                else:
                    pl.when(b_c == n_chunks - 1)(last_chunk)

    run(0, even, None, None)

    def pair(t2, carry):
        t = 2 * t2
        run(t + 1, odd, t, even)
        run(t + 2, even, t + 1, odd)
        return carry

    lax.fori_loop(0, n_items // 2 - 1, pair, 0)
    run(n_items - 1, odd, n_items - 2, even)
    run(None, None, n_items - 1, odd)


def _diff_attention(qkvf, lam_l, subln_g_l, band, batch, seq, lam_init):
    tq = _tile(seq, ATTN_ROWS_PER_STEP)
    assert tq % ATTN_TILE == 0 and seq % tq == 0
    tk = _tile(seq, ATTN_KV_CHUNK)
    assert tk % ATTN_TILE == 0
    n_chunks = seq // tk
    qkv3 = qkvf.reshape(batch, seq, qkvf.shape[-1])
    kern = functools.partial(_attn_kernel, n_blocks=tq // ATTN_TILE, n_chunks=n_chunks, tk=tk)
    out = pl.pallas_call(
        kern,
        grid=(batch, N_HEADS, seq // tq),
        in_specs=[
            pl.BlockSpec(memory_space=pltpu.SMEM),
            pl.BlockSpec((4, HEAD_DIM), lambda b, h, i: (0, 0)),
            pl.BlockSpec((1, V_DIM), lambda b, h, i: (0, 0)),
            pl.BlockSpec((None, tq, V_DIM), lambda b, h, i: (b, i, h)),
            pl.BlockSpec((None, seq, V_DIM), lambda b, h, i: (b, 0, N_HEADS + h)),
            pl.BlockSpec((None, seq, V_DIM), lambda b, h, i: (b, 0, 2 * N_HEADS + h)),
            pl.BlockSpec((None, N_BIAS_TILES, ATTN_TILE, ATTN_TILE), lambda b, h, i: (h, 0, 0, 0)),
        ],
        out_specs=pl.BlockSpec((None, tq, V_DIM), lambda b, h, i: (b, i, h)),
        out_shape=jax.ShapeDtypeStruct((batch, seq, N_HEADS * V_DIM), BF16),
        scratch_shapes=[
            pltpu.VMEM((2, ATTN_TILE, tk), F32), pltpu.VMEM((2, ATTN_TILE, tk), F32),
            pltpu.VMEM((2, ATTN_TILE, LANES), F32), pltpu.VMEM((2, ATTN_TILE, LANES), F32),
            pltpu.VMEM((2, ATTN_TILE, LANES), F32), pltpu.VMEM((2, ATTN_TILE, LANES), F32),
            pltpu.VMEM((2, ATTN_TILE, V_DIM), F32),
        ],
        compiler_params=_params("parallel", "parallel", "arbitrary"),
        name="diff_attention",
    )(jnp.full((1,), lam_init, F32), lam_l, subln_g_l.reshape(1, V_DIM), qkv3, qkv3, qkv3, band)
    return out.reshape(batch * seq, N_HEADS * V_DIM)


def _dft_tables(n, scale):
    j = lax.broadcasted_iota(jnp.int32, (n, n), 0)
    k = lax.broadcasted_iota(jnp.int32, (n, n), 1)
    ang = ((j * k) % n).astype(F32) * (2.0 * math.pi / n)
    return jnp.cos(ang) * scale, jnp.sin(ang) * scale


def _chan_dft_kernel(f_ref, cs_ref, o_ref):
    cs = cs_ref[...]
    for g in range(N_FGROUPS):
        lo = g * FGROUP_DIM
        z = jnp.dot(f_ref[:, lo:lo + FGROUP_DIM], cs, preferred_element_type=F32)
        o_ref[0, :, lo:lo + FGROUP_DIM] = z[:, :FGROUP_DIM].astype(o_ref.dtype)
        o_ref[1, :, lo:lo + FGROUP_DIM] = z[:, FGROUP_DIM:].astype(o_ref.dtype)


def _seq_dft_kernel(w_ref, z_ref, o_ref, acc_ref):
    k = pl.program_id(3)

    @pl.when(k == 0)
    def _():
        acc_ref[...] = jnp.zeros_like(acc_ref)

    acc_ref[...] += jnp.dot(w_ref[...], z_ref[...], preferred_element_type=F32)

    @pl.when(k == pl.num_programs(3) - 1)
    def _():
        o_ref[...] = acc_ref[...].astype(o_ref.dtype)


def _fourier_mix(qkvf, f_col_block, chan_cs, seq_w, batch, seq):
    fw = N_FGROUPS * FGROUP_DIM
    bm = _tile(seq, 1024)
    qkv3 = qkvf.reshape(batch, seq, qkvf.shape[-1])
    z = pl.pallas_call(
        _chan_dft_kernel,
        grid=(batch, seq // bm),
        in_specs=[
            pl.BlockSpec((None, bm, fw), lambda b, i: (b, i, f_col_block)),
            pl.BlockSpec((FGROUP_DIM, 2 * FGROUP_DIM), lambda b, i: (0, 0)),
        ],
        out_specs=pl.BlockSpec((None, 2, bm, fw), lambda b, i: (b, 0, i, 0)),
        out_shape=jax.ShapeDtypeStruct((batch, 2, seq, fw), BF16),
        compiler_params=_params("parallel", "parallel"),
        name="fnet_channel_dft",
    )(qkv3, chan_cs)
    z = z.reshape(batch, 2 * seq, fw)

    bn = _tile(fw, 1024)
    bk = _tile(2 * seq, 2048)
    y = pl.pallas_call(
        _seq_dft_kernel,
        grid=(batch, seq // bm, fw // bn, (2 * seq) // bk),
        in_specs=[
            pl.BlockSpec((bm, bk), lambda b, i, j, k: (i, k)),
            pl.BlockSpec((None, bk, bn), lambda b, i, j, k: (b, k, j)),
        ],
        out_specs=pl.BlockSpec((None, bm, bn), lambda b, i, j, k: (b, i, j)),
        out_shape=jax.ShapeDtypeStruct((batch, seq, fw), BF16),
        scratch_shapes=[pltpu.VMEM((bm, bn), F32)],
        compiler_params=_params("parallel", "parallel", "parallel", "arbitrary"),
        name="fnet_sequence_dft",
    )(seq_w, z)
    return y.reshape(batch * seq, fw)


def _merge_kernel(o_ref, f_ref, wa_ref, wf_ref, g1_ref, g2_ref, out_ref):
    ya = jnp.dot(o_ref[...], wa_ref[...], preferred_element_type=F32)
    yf = jnp.dot(f_ref[...], wf_ref[...], preferred_element_type=F32)
    out_ref[...] = (g1_ref[...].astype(F32) * ya + g2_ref[...].astype(F32) * yf).astype(out_ref.dtype)


def _gated_merge(o, fm, gates, w_attn, w_fnet, layer):
    t, ka = o.shape
    kf = fm.shape[1]
    d = w_attn.shape[-1]
    bm = _tile(t, 1024)
    bn = _tile(d, 512)
    nb = d // bn
    return pl.pallas_call(
        _merge_kernel,
        grid=(nb, t // bm),
        in_specs=[
            pl.BlockSpec((bm, ka), lambda j, i: (i, 0)),
            pl.BlockSpec((bm, kf), lambda j, i: (i, 0)),
            pl.BlockSpec((None, ka, bn), lambda j, i: (layer, 0, j)),
            pl.BlockSpec((None, kf, bn), lambda j, i: (layer, 0, j)),
            pl.BlockSpec((bm, bn), lambda j, i: (i, j)),
            pl.BlockSpec((bm, bn), lambda j, i: (i, nb + j)),
        ],
        out_specs=pl.BlockSpec((bm, bn), lambda j, i: (i, j)),
        out_shape=jax.ShapeDtypeStruct((t, d), BF16),
        compiler_params=_params("parallel", "parallel"),
        name="gated_merge",
    )(o, fm, w_attn, w_fnet, gates, gates)


def _proj_res_ln_kernel(a_ref, w_ref, x_ref, g_ref, b_ref, of_ref, ob_ref, acc_ref, *, alpha):
    k = pl.program_id(1)

    @pl.when(k == 0)
    def _():
        acc_ref[...] = jnp.zeros_like(acc_ref)

    acc_ref[...] += jnp.dot(a_ref[...], w_ref[...], preferred_element_type=F32)

    @pl.when(k == pl.num_programs(1) - 1)
    def _():
        y = _layernorm_rows(alpha * x_ref[...] + acc_ref[...], g_ref[...], b_ref[...])
        of_ref[...] = y
        ob_ref[...] = y.astype(BF16)


def _proj_residual_layernorm(a, w, layer, xf, g, b, alpha, bk_pref):
    t, kdim = a.shape
    d = w.shape[-1]
    bm = _tile(t, 512)
    bk = _tile(kdim, bk_pref)
    row = pl.BlockSpec((bm, d), lambda i, k: (i, 0))
    vec = pl.BlockSpec((None, 1, d), lambda i, k: (layer, 0, 0))
    return pl.pallas_call(
        functools.partial(_proj_res_ln_kernel, alpha=alpha),
        grid=(t // bm, kdim // bk),
        in_specs=[
            pl.BlockSpec((bm, bk), lambda i, k: (i, k)),
            pl.BlockSpec((None, bk, d), lambda i, k: (layer, k, 0)),
            row, vec, vec,
        ],
        out_specs=[row, row],
        out_shape=[jax.ShapeDtypeStruct((t, d), F32), jax.ShapeDtypeStruct((t, d), BF16)],
        scratch_shapes=[pltpu.VMEM((bm, d), F32)],
        compiler_params=_params("parallel", "arbitrary"),
        name="proj_residual_layernorm",
    )(a, w, xf, g, b)


def _swiglu_kernel(x_ref, wg_ref, wu_ref, o_ref):
    x = x_ref[...]
    gate = jnp.dot(x, wg_ref[...], preferred_element_type=F32)
    up = jnp.dot(x, wu_ref[...], preferred_element_type=F32)
    o_ref[...] = (gate * jax.nn.sigmoid(gate) * up).astype(o_ref.dtype)


def _swiglu_up(xb, w_gu, layer, d_ff):
    t, d = xb.shape
    bm = _tile(t, 1024)
    bn = _tile(d_ff, 512)
    nb = d_ff // bn
    return pl.pallas_call(
        _swiglu_kernel,
        grid=(nb, t // bm),
        in_specs=[
            pl.BlockSpec((bm, d), lambda j, i: (i, 0)),
            pl.BlockSpec((None, d, bn), lambda j, i: (layer, 0, j)),
            pl.BlockSpec((None, d, bn), lambda j, i: (layer, 0, nb + j)),
        ],
        out_specs=pl.BlockSpec((bm, bn), lambda j, i: (i, j)),
        out_shape=jax.ShapeDtypeStruct((t, d_ff), BF16),
        compiler_params=_params("parallel", "parallel"),
        name="swiglu_up",
    )(xb, w_gu, w_gu)


def _trunk(x, band, chan_cs, seq_w, p):
    batch, seq, d = x.shape
    depth = p["w_in"].shape[0]
    alpha = (2.0 * depth) ** 0.25
    qk_width = N_HEADS * 2 * HEAD_DIM
    attn_width = N_HEADS * V_DIM
    fnet_width = N_FGROUPS * FGROUP_DIM
    n_main = 2 * qk_width + attn_width + fnet_width
    assert (2 * qk_width + attn_width) % fnet_width == 0
    f_col_block = (2 * qk_width + attn_width) // fnet_width
    d_ff = p["w_down"].shape[1]

    col_scale = jnp.concatenate(
        [jnp.full((qk_width,), HEAD_DIM ** -0.5, F32), jnp.ones((n_main - qk_width,), F32)]).reshape(1, n_main)

    xf, xb = _input_layernorm(x.reshape(batch * seq, d), p["ln_in_g"], p["ln_in_b"])
    for l in range(depth):
        lam_init = 0.8 - 0.6 * math.exp(-0.3 * l)
        qkvf, gates = _in_projection(xb, p["w_in"], l, col_scale, p["b_gate"], n_main)
        o = _diff_attention(qkvf, p["lam"][l], p["subln_g"][l], band, batch, seq, lam_init)
        fm = _fourier_mix(qkvf, f_col_block, chan_cs, seq_w, batch, seq)
        merged = _gated_merge(o, fm, gates, p["w_br_attn"], p["w_br_fnet"], l)
        xf, xb = _proj_residual_layernorm(merged, p["w_out"], l, xf, p["ln1_g"], p["ln1_b"], alpha, 2048)
        act = _swiglu_up(xb, p["w_gu"], l, d_ff)
        xf, xb = _proj_residual_layernorm(act, p["w_down"], l, xf, p["ln2_g"], p["ln2_b"], alpha, 1408)
    return xf.reshape(batch, seq, d)


def kernel(x_prompt, x_sample, rel_bias, ln_in_g, ln_in_b, w_in, b_gate, lam, subln_g, w_br_attn, w_br_fnet,
           w_out, ln1_g, ln1_b, w_gu, w_down, ln2_g, ln2_b):
    depth, d = ln1_g.shape
    p = {
        "ln_in_g": ln_in_g, "ln_in_b": ln_in_b,
        "w_in": w_in.astype(BF16), "b_gate": b_gate.reshape(depth, 1, -1),
        "lam": lam, "subln_g": subln_g,
        "w_br_attn": w_br_attn.astype(BF16), "w_br_fnet": w_br_fnet.astype(BF16),
        "w_out": w_out.astype(BF16), "w_gu": w_gu.astype(BF16), "w_down": w_down.astype(BF16),
        "ln1_g": ln1_g.reshape(depth, 1, d), "ln1_b": ln1_b.reshape(depth, 1, d),
        "ln2_g": ln2_g.reshape(depth, 1, d), "ln2_b": ln2_b.reshape(depth, 1, d),
    }
    band = _bias_tiles(rel_bias)
    cc, sc = _dft_tables(FGROUP_DIM, FGROUP_DIM ** -0.5)
    chan_cs = jnp.concatenate([cc, sc], axis=1).astype(BF16)

    outs = []
    for x in (x_prompt, x_sample):
        seq = x.shape[1]
        cs, ss = _dft_tables(seq, seq ** -0.5)
        seq_w = jnp.concatenate([cs, -ss], axis=1).astype(BF16)
        outs.append(_trunk(x, band, chan_cs, seq_w, p))
    return tuple(outs)
```

```python
import functools
import math

import jax
import jax.numpy as jnp
from jax import lax
from jax.experimental import pallas as pl
from jax.experimental.pallas import tpu as pltpu

N_HEADS = 8
HEAD_DIM = 128
V_DIM = 2 * HEAD_DIM
N_FGROUPS = 8
FGROUP_DIM = 128
N_BUCKETS = 32
MAX_DISTANCE = 128
LN_EPS = 1e-5
LOG2_E = math.log2(math.e)

LANES = 128
ATTN_TILE = 256
ATTN_KV_CHUNK = 2048
ATTN_ROWS_PER_STEP = 2048
VMEM_LIMIT_BYTES = 56 * 1024 * 1024

F32 = jnp.float32
BF16 = jnp.bfloat16


def _tile(n, pref):
    if n <= pref:
        return n
    t = (pref // LANES) * LANES
    while t >= LANES:
        if n % t == 0:
            return t
        t -= LANES
    raise ValueError(f"no lane-aligned tile for {n}")


def _params(*sem, flags=None):
    return pltpu.CompilerParams(dimension_semantics=sem, vmem_limit_bytes=VMEM_LIMIT_BYTES, flags=flags)


def _layernorm_rows(y, g, b):
    mu = jnp.mean(y, axis=-1, keepdims=True)
    yc = y - mu
    var = jnp.mean(yc * yc, axis=-1, keepdims=True)
    return yc * lax.rsqrt(var + LN_EPS) * g + b


def _ln_kernel(x_ref, g_ref, b_ref, of_ref, ob_ref):
    y = _layernorm_rows(x_ref[...], g_ref[...], b_ref[...])
    of_ref[...] = y
    ob_ref[...] = y.astype(BF16)


def _input_layernorm(x, g, b):
    t, d = x.shape
    bm = _tile(t, 512)
    row = pl.BlockSpec((bm, d), lambda i: (i, 0))
    vec = pl.BlockSpec((1, d), lambda i: (0, 0))
    return pl.pallas_call(
        _ln_kernel,
        grid=(t // bm,),
        in_specs=[row, vec, vec],
        out_specs=[row, row],
        out_shape=[jax.ShapeDtypeStruct((t, d), F32), jax.ShapeDtypeStruct((t, d), BF16)],
        compiler_params=_params("parallel"),
        name="input_layernorm",
    )(x, g.reshape(1, d), b.reshape(1, d))


def _proj_kernel(x_ref, w_ref, s_ref, o_ref):
    acc = jnp.dot(x_ref[...], w_ref[...], preferred_element_type=F32)
    o_ref[...] = (acc * s_ref[...]).astype(o_ref.dtype)


def _gate_kernel(x_ref, w_ref, b_ref, o_ref):
    acc = jnp.dot(x_ref[...], w_ref[...], preferred_element_type=F32)
    o_ref[...] = jax.nn.sigmoid(acc + b_ref[...]).astype(o_ref.dtype)


def _in_projection(xb, w_in, layer, col_scale, b_gate, n_main):
    t, d = xb.shape
    n_gate = w_in.shape[-1] - n_main
    bm = _tile(t, 1024)

    bn = _tile(n_main, 1024)
    qkvf = pl.pallas_call(
        _proj_kernel,
        grid=(n_main // bn, t // bm),
        in_specs=[
            pl.BlockSpec((bm, d), lambda j, i: (i, 0)),
            pl.BlockSpec((None, d, bn), lambda j, i: (layer, 0, j)),
            pl.BlockSpec((1, bn), lambda j, i: (0, j)),
        ],
        out_specs=pl.BlockSpec((bm, bn), lambda j, i: (i, j)),
        out_shape=jax.ShapeDtypeStruct((t, n_main), BF16),
        compiler_params=_params("parallel", "parallel"),
        name="in_proj_qkvf",
    )(xb, w_in, col_scale)

    bg = _tile(n_gate, 1024)
    assert n_main % bg == 0
    off = n_main // bg
    gates = pl.pallas_call(
        _gate_kernel,
        grid=(n_gate // bg, t // bm),
        in_specs=[
            pl.BlockSpec((bm, d), lambda j, i: (i, 0)),
            pl.BlockSpec((None, d, bg), lambda j, i: (layer, 0, off + j)),
            pl.BlockSpec((None, 1, bg), lambda j, i: (layer, 0, j)),
        ],
        out_specs=pl.BlockSpec((bm, bg), lambda j, i: (i, j)),
        out_shape=jax.ShapeDtypeStruct((t, n_gate), BF16),
        compiler_params=_params("parallel", "parallel"),
        name="in_proj_gates",
    )(xb, w_in, b_gate)
    return qkvf, gates


def _t5_bucket(rel):
    nb = N_BUCKETS // 2
    ret = jnp.where(rel > 0, nb, 0)
    n = jnp.abs(rel)
    max_exact = nb // 2
    nf = jnp.maximum(n, 1).astype(F32)
    large = max_exact + (jnp.log(nf / max_exact) / math.log(MAX_DISTANCE / max_exact) * (nb - max_exact)).astype(jnp.int32)
    large = jnp.minimum(large, nb - 1)
    return ret + jnp.where(n < max_exact, n, large)


N_BIAS_TILES = 5


def _bias_tile_kernel(tab_ref, ids_ref, o_ref):
    h = pl.program_id(0)
    ids = ids_ref[...]
    acc = jnp.zeros(ids.shape, F32)
    for b in range(N_BUCKETS):
        acc = jnp.where(ids == b, tab_ref[b, h], acc)
    o_ref[...] = acc * LOG2_E


def _bias_tiles(rel_bias):
    tb = ATTN_TILE
    assert tb >= MAX_DISTANCE
    i = lax.broadcasted_iota(jnp.int32, (N_BIAS_TILES, tb, tb), 1)
    j = lax.broadcasted_iota(jnp.int32, (N_BIAS_TILES, tb, tb), 2)
    d = lax.broadcasted_iota(jnp.int32, (N_BIAS_TILES, tb, tb), 0) - 2
    ids = _t5_bucket(d * tb + j - i).reshape(N_BIAS_TILES * tb, tb)
    out = pl.pallas_call(
        _bias_tile_kernel,
        grid=(N_HEADS,),
        in_specs=[
            pl.BlockSpec(memory_space=pltpu.SMEM),
            pl.BlockSpec((N_BIAS_TILES * tb, tb), lambda h: (0, 0)),
        ],
        out_specs=pl.BlockSpec((None, N_BIAS_TILES * tb, tb), lambda h: (h, 0, 0)),
        out_shape=jax.ShapeDtypeStruct((N_HEADS, N_BIAS_TILES * tb, tb), F32),
        compiler_params=_params("parallel"),
        name="t5_bias_tiles",
    )(rel_bias, ids)
    return out.reshape(N_HEADS, N_BIAS_TILES, tb, tb)


def _lane_wide(x, width):
    return jnp.concatenate([x] * (width // LANES), axis=1)


def _attn_kernel(lam_init_ref, lam_ref, g_ref, q_ref, k_ref, v_ref, band_ref, o_ref,
                 s_ref, mx_ref, m_ref, l_ref, acc_ref, *, n_blocks, n_chunks, tk):
    tq = ATTN_TILE
    sub = tk // ATTN_TILE
    n_items = n_blocks * n_chunks
    assert n_items >= 2
    group = pl.program_id(2)
    nt = (((1,), (1,)), ((), ()))

    lam_init = lam_init_ref[0]
    lam = lam_ref[...]
    lam_full = (jnp.exp(jnp.sum(lam[0:1] * lam[1:2], axis=-1, keepdims=True))
                - jnp.exp(jnp.sum(lam[2:3] * lam[3:4], axis=-1, keepdims=True)) + lam_init)
    gain = g_ref[...] * (1.0 - lam_init)

    if n_chunks > 1:
        m_ref[...] = jnp.zeros(m_ref.shape, F32)
        l_ref[...] = jnp.zeros(l_ref.shape, F32)
        acc_ref[...] = jnp.zeros(acc_ref.shape, F32)

    def position(t):
        if isinstance(t, int):
            return t // n_chunks, t % n_chunks
        return lax.div(t, n_chunks), lax.rem(t, n_chunks)

    def rows_of(i):
        return pl.ds(i * tq, tq) if isinstance(i, int) else pl.ds(pl.multiple_of(i * tq, tq), tq)

    def keys_of(c, u):
        start = c * tk + u * ATTN_TILE
        return pl.ds(start if isinstance(c, int) else pl.multiple_of(start, ATTN_TILE), ATTN_TILE)

    def half_max(x):
        return jnp.maximum(x[:, :LANES], x[:, LANES:])

    def half_sum(x):
        return x[:, :LANES] + x[:, LANES:]

    def finish_rows(i, o1, o2):
        of = o1 - lam_full * o2
        of = of * lax.rsqrt(jnp.mean(of * of, axis=-1, keepdims=True) + LN_EPS)
        o_ref[rows_of(i), :] = (of * gain).astype(o_ref.dtype)

    def run(a_item, b_item):
        if a_item is not None:
            a_i, a_c = position(a_item)
            a_row = group * n_blocks + a_i
            a_q = [q_ref[rows_of(a_i), mi * HEAD_DIM:(mi + 1) * HEAD_DIM] for mi in range(2)]
            a_max = [None, None]
        if b_item is not None:
            b_i, b_c = position(b_item)
            if n_chunks == 1:
                b_m = [mx_ref[mi] for mi in range(2)]
            else:
                b_m_prev = [jnp.where(b_c == 0, -jnp.inf, m_ref[mi]) for mi in range(2)]
                b_m = [jnp.maximum(b_m_prev[mi], mx_ref[mi]) for mi in range(2)]
                b_alpha = [jnp.exp2(b_m_prev[mi] - b_m[mi]) for mi in range(2)]
            b_m_wide = [_lane_wide(b_m[mi], ATTN_TILE) for mi in range(2)]
            b_sum = [None, None]
            b_acc = [None, None]

        for u in range(sub):
            cols = slice(u * ATTN_TILE, (u + 1) * ATTN_TILE)
            if b_item is not None:
                vu = v_ref[keys_of(b_c, u), :]
                for mi in range(2):
                    p = jnp.exp2(s_ref[mi, :, cols] - b_m_wide[mi])
                    b_sum[mi] = half_sum(p) if b_sum[mi] is None else b_sum[mi] + half_sum(p)
                    pv = jnp.dot(p.astype(BF16), vu, preferred_element_type=F32)
                    b_acc[mi] = pv if b_acc[mi] is None else b_acc[mi] + pv
            if a_item is not None:
                tile = band_ref[jnp.clip(a_c * sub + u - a_row, -2, 2) + 2]
                for mi in range(2):
                    s = lax.dot_general(a_q[mi], k_ref[keys_of(a_c, u), mi * HEAD_DIM:(mi + 1) * HEAD_DIM],
                                        nt, preferred_element_type=F32) + tile
                    s_ref[mi, :, cols] = s
                    a_max[mi] = half_max(s) if a_max[mi] is None else jnp.maximum(a_max[mi], half_max(s))

        if a_item is not None:
            for mi in range(2):
                mx_ref[mi] = jnp.broadcast_to(jnp.max(a_max[mi], axis=-1, keepdims=True), (tq, LANES))
        if b_item is not None:
            b_l = [jnp.sum(b_sum[mi], axis=-1, keepdims=True) for mi in range(2)]
            if n_chunks == 1:
                finish_rows(b_i, b_acc[0] / b_l[0], b_acc[1] / b_l[1])
            else:
                for mi in range(2):
                    l_ref[mi] = b_alpha[mi] * l_ref[mi] + b_l[mi]
                    m_ref[mi] = b_m[mi]
                    acc_ref[mi] = _lane_wide(b_alpha[mi], V_DIM) * acc_ref[mi] + b_acc[mi]

                def last_chunk():
                    finish_rows(b_i, acc_ref[0] / _lane_wide(l_ref[0], V_DIM),
                                acc_ref[1] / _lane_wide(l_ref[1], V_DIM))
                if isinstance(b_c, int):
                    if b_c == n_chunks - 1:
                        last_chunk()
                else:
                    pl.when(b_c == n_chunks - 1)(last_chunk)

    run(0, None)

    def body(t, carry):
        run(t + 1, t)
        return carry

    lax.fori_loop(0, n_items - 1, body, 0)
    run(None, n_items - 1)


def _diff_attention(qkvf, lam_l, subln_g_l, band, batch, seq, lam_init):
    tq = _tile(seq, ATTN_ROWS_PER_STEP)
    assert tq % ATTN_TILE == 0 and seq % tq == 0
    tk = _tile(seq, ATTN_KV_CHUNK)
    assert tk % ATTN_TILE == 0
    n_chunks = seq // tk
    qkv3 = qkvf.reshape(batch, seq, qkvf.shape[-1])
    kern = functools.partial(_attn_kernel, n_blocks=tq // ATTN_TILE, n_chunks=n_chunks, tk=tk)
    out = pl.pallas_call(
        kern,
        grid=(batch, N_HEADS, seq // tq),
        in_specs=[
            pl.BlockSpec(memory_space=pltpu.SMEM),
            pl.BlockSpec((4, HEAD_DIM), lambda b, h, i: (0, 0)),
            pl.BlockSpec((1, V_DIM), lambda b, h, i: (0, 0)),
            pl.BlockSpec((None, tq, V_DIM), lambda b, h, i: (b, i, h)),
            pl.BlockSpec((None, seq, V_DIM), lambda b, h, i: (b, 0, N_HEADS + h)),
            pl.BlockSpec((None, seq, V_DIM), lambda b, h, i: (b, 0, 2 * N_HEADS + h)),
            pl.BlockSpec((None, N_BIAS_TILES, ATTN_TILE, ATTN_TILE), lambda b, h, i: (h, 0, 0, 0)),
        ],
        out_specs=pl.BlockSpec((None, tq, V_DIM), lambda b, h, i: (b, i, h)),
        out_shape=jax.ShapeDtypeStruct((batch, seq, N_HEADS * V_DIM), BF16),
        scratch_shapes=[
            pltpu.VMEM((2, ATTN_TILE, tk), F32), pltpu.VMEM((2, ATTN_TILE, LANES), F32),
            pltpu.VMEM((2, ATTN_TILE, LANES), F32), pltpu.VMEM((2, ATTN_TILE, LANES), F32),
            pltpu.VMEM((2, ATTN_TILE, V_DIM), F32),
        ],
        compiler_params=_params("parallel", "parallel", "arbitrary"),
        name="diff_attention",
    )(jnp.full((1,), lam_init, F32), lam_l, subln_g_l.reshape(1, V_DIM), qkv3, qkv3, qkv3, band)
    return out.reshape(batch * seq, N_HEADS * V_DIM)


FNET_S2 = 128
FNET_K1_GROUP = 8


def _angles(j, k, n):
    return ((j * k) % n).astype(F32) * (2.0 * math.pi / n)


def _chan_dft_table():
    n = FGROUP_DIM
    ang = _angles(lax.broadcasted_iota(jnp.int32, (n, n), 0), lax.broadcasted_iota(jnp.int32, (n, n), 1), n)
    return (jnp.concatenate([jnp.cos(ang), jnp.sin(ang)], axis=1) * n ** -0.5).astype(BF16)


def _seq_dft_tables(seq):
    s2 = min(FNET_S2, seq)
    s1 = seq // s2
    a1 = _angles(lax.broadcasted_iota(jnp.int32, (s1, s1), 0), lax.broadcasted_iota(jnp.int32, (s1, s1), 1), s1)
    c1, sn1 = jnp.cos(a1), jnp.sin(a1)
    m1 = jnp.concatenate([jnp.concatenate([c1, -sn1], axis=1), jnp.concatenate([-sn1, -c1], axis=1)], axis=0)
    k1 = lax.broadcasted_iota(jnp.int32, (s1, s2, s2), 0)
    k2 = lax.broadcasted_iota(jnp.int32, (s1, s2, s2), 1)
    n2 = lax.broadcasted_iota(jnp.int32, (s1, s2, s2), 2)
    a2 = _angles(k1 + s1 * k2, n2, seq)
    g = jnp.concatenate([jnp.cos(a2), jnp.sin(a2)], axis=2)
    return (m1 * s1 ** -0.5).astype(BF16), (g * s2 ** -0.5).astype(BF16)


def _chan_dft_kernel(f_ref, cs_ref, o_ref):
    cs = cs_ref[...]
    for g in range(N_FGROUPS):
        lo = g * FGROUP_DIM
        z = jnp.dot(f_ref[:, lo:lo + FGROUP_DIM], cs, preferred_element_type=F32)
        o_ref[0, :, lo:lo + FGROUP_DIM] = z[:, :FGROUP_DIM].astype(o_ref.dtype)
        o_ref[1, :, lo:lo + FGROUP_DIM] = z[:, FGROUP_DIM:].astype(o_ref.dtype)


def _seq_stage1_kernel(m_ref, z_ref, o_ref):
    o_ref[...] = jnp.dot(m_ref[...], z_ref[...], preferred_element_type=F32).astype(o_ref.dtype)


def _seq_stage2_kernel(g_ref, a_ref, o_ref, *, group, width):
    for j in range(group):
        a = a_ref[:, j].reshape(2 * g_ref.shape[1], width)
        o_ref[:, j * width:(j + 1) * width] = jnp.dot(g_ref[j], a, preferred_element_type=F32).astype(o_ref.dtype)


def _fourier_mix(qkvf, f_col_block, chan_cs, seq_tabs, batch, seq):
    fw = N_FGROUPS * FGROUP_DIM
    m1, g2 = seq_tabs
    s1, s2 = g2.shape[0], g2.shape[1]
    bm = _tile(seq, 1024)
    qkv3 = qkvf.reshape(batch, seq, qkvf.shape[-1])
    z = pl.pallas_call(
        _chan_dft_kernel,
        grid=(batch, seq // bm),
        in_specs=[
            pl.BlockSpec((None, bm, fw), lambda b, i: (b, i, f_col_block)),
            pl.BlockSpec((FGROUP_DIM, 2 * FGROUP_DIM), lambda b, i: (0, 0)),
        ],
        out_specs=pl.BlockSpec((None, 2, bm, fw), lambda b, i: (b, 0, i, 0)),
        out_shape=jax.ShapeDtypeStruct((batch, 2, seq, fw), BF16),
        compiler_params=_params("parallel", "parallel"),
        name="fnet_channel_dft",
    )(qkv3, chan_cs)

    ncol = s2 * fw
    bn = _tile(ncol, 8192)
    a = pl.pallas_call(
        _seq_stage1_kernel,
        grid=(batch, ncol // bn),
        in_specs=[
            pl.BlockSpec((2 * s1, 2 * s1), lambda b, j: (0, 0)),
            pl.BlockSpec((None, 2 * s1, bn), lambda b, j: (b, 0, j)),
        ],
        out_specs=pl.BlockSpec((None, 2 * s1, bn), lambda b, j: (b, 0, j)),
        out_shape=jax.ShapeDtypeStruct((batch, 2 * s1, ncol), BF16),
        compiler_params=_params("parallel", "parallel"),
        name="fnet_seq_dft_stage1",
    )(m1, z.reshape(batch, 2 * s1, ncol))

    gk = min(FNET_K1_GROUP, s1)
    assert s1 % gk == 0
    y = pl.pallas_call(
        functools.partial(_seq_stage2_kernel, group=gk, width=fw),
        grid=(batch, s1 // gk),
        in_specs=[
            pl.BlockSpec((gk, s2, 2 * s2), lambda b, j: (j, 0, 0)),
            pl.BlockSpec((None, 2, gk, s2, fw), lambda b, j: (b, 0, j, 0, 0)),
        ],
        out_specs=pl.BlockSpec((None, s2, gk * fw), lambda b, j: (b, 0, j)),
        out_shape=jax.ShapeDtypeStruct((batch, s2, s1 * fw), BF16),
        compiler_params=_params("parallel", "parallel"),
        name="fnet_seq_dft_stage2",
    )(g2, a.reshape(batch, 2, s1, s2, fw))
    return y.reshape(batch * seq, fw)


def _merge_kernel(o_ref, f_ref, wa_ref, wf_ref, g1_ref, g2_ref, out_ref):
    ya = jnp.dot(o_ref[...], wa_ref[...], preferred_element_type=F32)
    yf = jnp.dot(f_ref[...], wf_ref[...], preferred_element_type=F32)
    out_ref[...] = (g1_ref[...].astype(F32) * ya + g2_ref[...].astype(F32) * yf).astype(out_ref.dtype)


def _gated_merge(o, fm, gates, w_attn, w_fnet, layer):
    t, ka = o.shape
    kf = fm.shape[1]
    d = w_attn.shape[-1]
    bm = _tile(t, 1024)
    bn = _tile(d, 512)
    nb = d // bn
    return pl.pallas_call(
        _merge_kernel,
        grid=(nb, t // bm),
        in_specs=[
            pl.BlockSpec((bm, ka), lambda j, i: (i, 0)),
            pl.BlockSpec((bm, kf), lambda j, i: (i, 0)),
            pl.BlockSpec((None, ka, bn), lambda j, i: (layer, 0, j)),
            pl.BlockSpec((None, kf, bn), lambda j, i: (layer, 0, j)),
            pl.BlockSpec((bm, bn), lambda j, i: (i, j)),
            pl.BlockSpec((bm, bn), lambda j, i: (i, nb + j)),
        ],
        out_specs=pl.BlockSpec((bm, bn), lambda j, i: (i, j)),
        out_shape=jax.ShapeDtypeStruct((t, d), BF16),
        compiler_params=_params("parallel", "parallel"),
        name="gated_merge",
    )(o, fm, w_attn, w_fnet, gates, gates)


def _proj_res_ln_kernel(a_ref, w_ref, x_ref, g_ref, b_ref, of_ref, ob_ref, acc_ref, *, alpha):
    k = pl.program_id(1)

    @pl.when(k == 0)
    def _():
        acc_ref[...] = jnp.zeros_like(acc_ref)

    acc_ref[...] += jnp.dot(a_ref[...], w_ref[...], preferred_element_type=F32)

    @pl.when(k == pl.num_programs(1) - 1)
    def _():
        y = _layernorm_rows(alpha * x_ref[...] + acc_ref[...], g_ref[...], b_ref[...])
        of_ref[...] = y
        ob_ref[...] = y.astype(BF16)


def _proj_residual_layernorm(a, w, layer, xf, g, b, alpha, bk_pref):
    t, kdim = a.shape
    d = w.shape[-1]
    bm = _tile(t, 512)
    bk = _tile(kdim, bk_pref)
    row = pl.BlockSpec((bm, d), lambda i, k: (i, 0))
    vec = pl.BlockSpec((None, 1, d), lambda i, k: (layer, 0, 0))
    return pl.pallas_call(
        functools.partial(_proj_res_ln_kernel, alpha=alpha),
        grid=(t // bm, kdim // bk),
        in_specs=[
            pl.BlockSpec((bm, bk), lambda i, k: (i, k)),
            pl.BlockSpec((None, bk, d), lambda i, k: (layer, k, 0)),
            row, vec, vec,
        ],
        out_specs=[row, row],
        out_shape=[jax.ShapeDtypeStruct((t, d), F32), jax.ShapeDtypeStruct((t, d), BF16)],
        scratch_shapes=[pltpu.VMEM((bm, d), F32)],
        compiler_params=_params("parallel", "arbitrary"),
        name="proj_residual_layernorm",
    )(a, w, xf, g, b)


def _swiglu_kernel(x_ref, wg_ref, wu_ref, o_ref):
    x = x_ref[...]
    gate = jnp.dot(x, wg_ref[...], preferred_element_type=F32)
    up = jnp.dot(x, wu_ref[...], preferred_element_type=F32)
    o_ref[...] = (gate * jax.nn.sigmoid(gate) * up).astype(o_ref.dtype)


def _swiglu_up(xb, w_gu, layer, d_ff):
    t, d = xb.shape
    bm = _tile(t, 1024)
    bn = _tile(d_ff, 512)
    nb = d_ff // bn
    return pl.pallas_call(
        _swiglu_kernel,
        grid=(nb, t // bm),
        in_specs=[
            pl.BlockSpec((bm, d), lambda j, i: (i, 0)),
            pl.BlockSpec((None, d, bn), lambda j, i: (layer, 0, j)),
            pl.BlockSpec((None, d, bn), lambda j, i: (layer, 0, nb + j)),
        ],
        out_specs=pl.BlockSpec((bm, bn), lambda j, i: (i, j)),
        out_shape=jax.ShapeDtypeStruct((t, d_ff), BF16),
        compiler_params=_params("parallel", "parallel"),
        name="swiglu_up",
    )(xb, w_gu, w_gu)


def _trunk(x, band, chan_cs, seq_tabs, p):
    batch, seq, d = x.shape
    depth = p["w_in"].shape[0]
    alpha = (2.0 * depth) ** 0.25
    qk_width = N_HEADS * 2 * HEAD_DIM
    attn_width = N_HEADS * V_DIM
    fnet_width = N_FGROUPS * FGROUP_DIM
    n_main = 2 * qk_width + attn_width + fnet_width
    assert (2 * qk_width + attn_width) % fnet_width == 0
    f_col_block = (2 * qk_width + attn_width) // fnet_width
    d_ff = p["w_down"].shape[1]

    col_scale = jnp.concatenate(
        [jnp.full((qk_width,), HEAD_DIM ** -0.5 * LOG2_E, F32), jnp.ones((n_main - qk_width,), F32)]).reshape(1, n_main)

    xf, xb = _input_layernorm(x.reshape(batch * seq, d), p["ln_in_g"], p["ln_in_b"])
    for l in range(depth):
        lam_init = 0.8 - 0.6 * math.exp(-0.3 * l)
        qkvf, gates = _in_projection(xb, p["w_in"], l, col_scale, p["b_gate"], n_main)
        o = _diff_attention(qkvf, p["lam"][l], p["subln_g"][l], band, batch, seq, lam_init)
        fm = _fourier_mix(qkvf, f_col_block, chan_cs, seq_tabs, batch, seq)
        merged = _gated_merge(o, fm, gates, p["w_br_attn"], p["w_br_fnet"], l)
        xf, xb = _proj_residual_layernorm(merged, p["w_out"], l, xf, p["ln1_g"], p["ln1_b"], alpha, 2048)
        act = _swiglu_up(xb, p["w_gu"], l, d_ff)
        xf, xb = _proj_residual_layernorm(act, p["w_down"], l, xf, p["ln2_g"], p["ln2_b"], alpha, 1408)
    return xf.reshape(batch, seq, d)


def kernel(x_prompt, x_sample, rel_bias, ln_in_g, ln_in_b, w_in, b_gate, lam, subln_g, w_br_attn, w_br_fnet,
           w_out, ln1_g, ln1_b, w_gu, w_down, ln2_g, ln2_b):
    depth, d = ln1_g.shape
    p = {
        "ln_in_g": ln_in_g, "ln_in_b": ln_in_b,
        "w_in": w_in.astype(BF16), "b_gate": b_gate.reshape(depth, 1, -1),
        "lam": lam, "subln_g": subln_g,
        "w_br_attn": w_br_attn.astype(BF16), "w_br_fnet": w_br_fnet.astype(BF16),
        "w_out": w_out.astype(BF16), "w_gu": w_gu.astype(BF16), "w_down": w_down.astype(BF16),
        "ln1_g": ln1_g.reshape(depth, 1, d), "ln1_b": ln1_b.reshape(depth, 1, d),
        "ln2_g": ln2_g.reshape(depth, 1, d), "ln2_b": ln2_b.reshape(depth, 1, d),
    }
    band = _bias_tiles(rel_bias)
    chan_cs = _chan_dft_table()
    return tuple(_trunk(x, band, chan_cs, _seq_dft_tables(x.shape[1]), p) for x in (x_prompt, x_sample))
```

```python
import functools
import math

import jax
import jax.numpy as jnp
from jax import lax
from jax.experimental import pallas as pl
from jax.experimental.pallas import tpu as pltpu

N_HEADS = 8
HEAD_DIM = 128
V_DIM = 2 * HEAD_DIM
N_FGROUPS = 8
FGROUP_DIM = 128
N_BUCKETS = 32
MAX_DISTANCE = 128
LN_EPS = 1e-5
LOG2_E = math.log2(math.e)

LANES = 128
ATTN_TILE = 256
ATTN_KV_CHUNK = 8192
ATTN_ROWS_PER_STEP = 2048
VMEM_LIMIT_BYTES = 56 * 1024 * 1024

F32 = jnp.float32
BF16 = jnp.bfloat16


def _tile(n, pref):
    if n <= pref:
        return n
    t = (pref // LANES) * LANES
    while t >= LANES:
        if n % t == 0:
            return t
        t -= LANES
    raise ValueError(f"no lane-aligned tile for {n}")


def _params(*sem, flags=None):
    return pltpu.CompilerParams(dimension_semantics=sem, vmem_limit_bytes=VMEM_LIMIT_BYTES, flags=flags)


def _layernorm_rows(y, g, b):
    mu = jnp.mean(y, axis=-1, keepdims=True)
    yc = y - mu
    var = jnp.mean(yc * yc, axis=-1, keepdims=True)
    return yc * lax.rsqrt(var + LN_EPS) * g + b


def _ln_kernel(x_ref, g_ref, b_ref, of_ref, ob_ref):
    y = _layernorm_rows(x_ref[...], g_ref[...], b_ref[...])
    of_ref[...] = y
    ob_ref[...] = y.astype(BF16)


def _input_layernorm(x, g, b):
    t, d = x.shape
    bm = _tile(t, 512)
    row = pl.BlockSpec((bm, d), lambda i: (i, 0))
    vec = pl.BlockSpec((1, d), lambda i: (0, 0))
    return pl.pallas_call(
        _ln_kernel,
        grid=(t // bm,),
        in_specs=[row, vec, vec],
        out_specs=[row, row],
        out_shape=[jax.ShapeDtypeStruct((t, d), F32), jax.ShapeDtypeStruct((t, d), BF16)],
        compiler_params=_params("parallel"),
        name="input_layernorm",
    )(x, g.reshape(1, d), b.reshape(1, d))


def _proj_kernel(x_ref, w_ref, s_ref, o_ref):
    acc = jnp.dot(x_ref[...], w_ref[...], preferred_element_type=F32)
    o_ref[...] = (acc * s_ref[...]).astype(o_ref.dtype)


def _gate_kernel(x_ref, w_ref, b_ref, o_ref):
    acc = jnp.dot(x_ref[...], w_ref[...], preferred_element_type=F32)
    o_ref[...] = jax.nn.sigmoid(acc + b_ref[...]).astype(o_ref.dtype)


def _in_projection(xb, w_in, layer, col_scale, b_gate, n_main):
    t, d = xb.shape
    n_gate = w_in.shape[-1] - n_main
    bm = _tile(t, 1024)

    bn = _tile(n_main, 1024)
    qkvf = pl.pallas_call(
        _proj_kernel,
        grid=(n_main // bn, t // bm),
        in_specs=[
            pl.BlockSpec((bm, d), lambda j, i: (i, 0)),
            pl.BlockSpec((None, d, bn), lambda j, i: (layer, 0, j)),
            pl.BlockSpec((1, bn), lambda j, i: (0, j)),
        ],
        out_specs=pl.BlockSpec((bm, bn), lambda j, i: (i, j)),
        out_shape=jax.ShapeDtypeStruct((t, n_main), BF16),
        compiler_params=_params("parallel", "parallel"),
        name="in_proj_qkvf",
    )(xb, w_in, col_scale)

    bg = _tile(n_gate, 1024)
    assert n_main % bg == 0
    off = n_main // bg
    gates = pl.pallas_call(
        _gate_kernel,
        grid=(n_gate // bg, t // bm),
        in_specs=[
            pl.BlockSpec((bm, d), lambda j, i: (i, 0)),
            pl.BlockSpec((None, d, bg), lambda j, i: (layer, 0, off + j)),
            pl.BlockSpec((None, 1, bg), lambda j, i: (layer, 0, j)),
        ],
        out_specs=pl.BlockSpec((bm, bg), lambda j, i: (i, j)),
        out_shape=jax.ShapeDtypeStruct((t, n_gate), BF16),
        compiler_params=_params("parallel", "parallel"),
        name="in_proj_gates",
    )(xb, w_in, b_gate)
    return qkvf, gates


def _t5_bucket(rel):
    nb = N_BUCKETS // 2
    ret = jnp.where(rel > 0, nb, 0)
    n = jnp.abs(rel)
    max_exact = nb // 2
    nf = jnp.maximum(n, 1).astype(F32)
    large = max_exact + (jnp.log(nf / max_exact) / math.log(MAX_DISTANCE / max_exact) * (nb - max_exact)).astype(jnp.int32)
    large = jnp.minimum(large, nb - 1)
    return ret + jnp.where(n < max_exact, n, large)


N_BIAS_TILES = 5


def _bias_tile_kernel(tab_ref, ids_ref, o_ref):
    h = pl.program_id(0)
    ids = ids_ref[...]
    acc = jnp.zeros(ids.shape, F32)
    for b in range(N_BUCKETS):
        acc = jnp.where(ids == b, tab_ref[b, h], acc)
    o_ref[...] = acc * LOG2_E


def _bias_tiles(rel_bias):
    tb = ATTN_TILE
    assert tb >= MAX_DISTANCE
    i = lax.broadcasted_iota(jnp.int32, (N_BIAS_TILES, tb, tb), 1)
    j = lax.broadcasted_iota(jnp.int32, (N_BIAS_TILES, tb, tb), 2)
    d = lax.broadcasted_iota(jnp.int32, (N_BIAS_TILES, tb, tb), 0) - 2
    ids = _t5_bucket(d * tb + j - i).reshape(N_BIAS_TILES * tb, tb)
    out = pl.pallas_call(
        _bias_tile_kernel,
        grid=(N_HEADS,),
        in_specs=[
            pl.BlockSpec(memory_space=pltpu.SMEM),
            pl.BlockSpec((N_BIAS_TILES * tb, tb), lambda h: (0, 0)),
        ],
        out_specs=pl.BlockSpec((None, N_BIAS_TILES * tb, tb), lambda h: (h, 0, 0)),
        out_shape=jax.ShapeDtypeStruct((N_HEADS, N_BIAS_TILES * tb, tb), F32),
        compiler_params=_params("parallel"),
        name="t5_bias_tiles",
    )(rel_bias, ids)
    return out.reshape(N_HEADS, N_BIAS_TILES, tb, tb)


def _lane_wide(x, width):
    return jnp.concatenate([x] * (width // LANES), axis=1)


def _attn_kernel(lam_init_ref, lam_ref, g_ref, q_ref, k_ref, v_ref, band_ref, o_ref,
                 s_ref, mx_ref, m_ref, l_ref, acc_ref, *, n_blocks, n_chunks, tk):
    tq = ATTN_TILE
    sub = tk // ATTN_TILE
    n_items = n_blocks * n_chunks
    assert n_items >= 2
    group = pl.program_id(2)
    nt = (((1,), (1,)), ((), ()))

    lam_init = lam_init_ref[0]
    lam = lam_ref[...]
    lam_full = (jnp.exp(jnp.sum(lam[0:1] * lam[1:2], axis=-1, keepdims=True))
                - jnp.exp(jnp.sum(lam[2:3] * lam[3:4], axis=-1, keepdims=True)) + lam_init)
    gain = g_ref[...] * (1.0 - lam_init)

    if n_chunks > 1:
        m_ref[...] = jnp.zeros(m_ref.shape, F32)
        l_ref[...] = jnp.zeros(l_ref.shape, F32)
        acc_ref[...] = jnp.zeros(acc_ref.shape, F32)

    def position(t):
        if isinstance(t, int):
            return t // n_chunks, t % n_chunks
        return lax.div(t, n_chunks), lax.rem(t, n_chunks)

    def rows_of(i):
        return pl.ds(i * tq, tq) if isinstance(i, int) else pl.ds(pl.multiple_of(i * tq, tq), tq)

    def keys_of(c, u):
        start = c * tk + u * ATTN_TILE
        return pl.ds(start if isinstance(c, int) else pl.multiple_of(start, ATTN_TILE), ATTN_TILE)

    def half_max(x):
        return jnp.maximum(x[:, :LANES], x[:, LANES:])

    def half_sum(x):
        return x[:, :LANES] + x[:, LANES:]

    def finish_rows(i, o1, o2):
        of = o1 - lam_full * o2
        of = of * lax.rsqrt(jnp.mean(of * of, axis=-1, keepdims=True) + LN_EPS)
        o_ref[rows_of(i), :] = (of * gain).astype(o_ref.dtype)

    def run(a_item, b_item):
        if a_item is not None:
            a_i, a_c = position(a_item)
            a_row = group * n_blocks + a_i
            a_q = [q_ref[rows_of(a_i), mi * HEAD_DIM:(mi + 1) * HEAD_DIM] for mi in range(2)]
            a_max = [None, None]
        if b_item is not None:
            b_i, b_c = position(b_item)
            if n_chunks == 1:
                b_m = [mx_ref[mi] for mi in range(2)]
            else:
                b_m_prev = [jnp.where(b_c == 0, -jnp.inf, m_ref[mi]) for mi in range(2)]
                b_m = [jnp.maximum(b_m_prev[mi], mx_ref[mi]) for mi in range(2)]
                b_alpha = [jnp.exp2(b_m_prev[mi] - b_m[mi]) for mi in range(2)]
            b_m_wide = [_lane_wide(b_m[mi], ATTN_TILE) for mi in range(2)]
            b_sum = [None, None]
            b_acc = [None, None]

        for u in range(sub):
            cols = slice(u * ATTN_TILE, (u + 1) * ATTN_TILE)
            if b_item is not None:
                vu = v_ref[keys_of(b_c, u), :]
                for mi in range(2):
                    p = jnp.exp2(s_ref[mi, :, cols] - b_m_wide[mi])
                    b_sum[mi] = half_sum(p) if b_sum[mi] is None else b_sum[mi] + half_sum(p)
                    pv = jnp.dot(p.astype(BF16), vu, preferred_element_type=F32)
                    b_acc[mi] = pv if b_acc[mi] is None else b_acc[mi] + pv
            if a_item is not None:
                tile = band_ref[jnp.clip(a_c * sub + u - a_row, -2, 2) + 2]
                for mi in range(2):
                    s = lax.dot_general(a_q[mi], k_ref[keys_of(a_c, u), mi * HEAD_DIM:(mi + 1) * HEAD_DIM],
                                        nt, preferred_element_type=F32) + tile
                    s_ref[mi, :, cols] = s
                    a_max[mi] = half_max(s) if a_max[mi] is None else jnp.maximum(a_max[mi], half_max(s))

        if a_item is not None:
            for mi in range(2):
                mx_ref[mi] = jnp.broadcast_to(jnp.max(a_max[mi], axis=-1, keepdims=True), (tq, LANES))
        if b_item is not None:
            b_l = [jnp.sum(b_sum[mi], axis=-1, keepdims=True) for mi in range(2)]
            if n_chunks == 1:
                finish_rows(b_i, b_acc[0] / b_l[0], b_acc[1] / b_l[1])
            else:
                for mi in range(2):
                    l_ref[mi] = b_alpha[mi] * l_ref[mi] + b_l[mi]
                    m_ref[mi] = b_m[mi]
                    acc_ref[mi] = _lane_wide(b_alpha[mi], V_DIM) * acc_ref[mi] + b_acc[mi]

                def last_chunk():
                    finish_rows(b_i, acc_ref[0] / _lane_wide(l_ref[0], V_DIM),
                                acc_ref[1] / _lane_wide(l_ref[1], V_DIM))
                if isinstance(b_c, int):
                    if b_c == n_chunks - 1:
                        last_chunk()
                else:
                    pl.when(b_c == n_chunks - 1)(last_chunk)

    run(0, None)

    def body(t, carry):
        run(t + 1, t)
        return carry

    lax.fori_loop(0, n_items - 1, body, 0)
    run(None, n_items - 1)


def _diff_attention(qkvf, lam_l, subln_g_l, band, batch, seq, lam_init):
    tq = _tile(seq, ATTN_ROWS_PER_STEP)
    assert tq % ATTN_TILE == 0 and seq % tq == 0
    tk = _tile(seq, ATTN_KV_CHUNK)
    assert tk % ATTN_TILE == 0
    n_chunks = seq // tk
    qkv3 = qkvf.reshape(batch, seq, qkvf.shape[-1])
    kern = functools.partial(_attn_kernel, n_blocks=tq // ATTN_TILE, n_chunks=n_chunks, tk=tk)
    out = pl.pallas_call(
        kern,
        grid=(batch, N_HEADS, seq // tq),
        in_specs=[
            pl.BlockSpec(memory_space=pltpu.SMEM),
            pl.BlockSpec((4, HEAD_DIM), lambda b, h, i: (0, 0)),
            pl.BlockSpec((1, V_DIM), lambda b, h, i: (0, 0)),
            pl.BlockSpec((None, tq, V_DIM), lambda b, h, i: (b, i, h)),
            pl.BlockSpec((None, seq, V_DIM), lambda b, h, i: (b, 0, N_HEADS + h)),
            pl.BlockSpec((None, seq, V_DIM), lambda b, h, i: (b, 0, 2 * N_HEADS + h)),
            pl.BlockSpec((None, N_BIAS_TILES, ATTN_TILE, ATTN_TILE), lambda b, h, i: (h, 0, 0, 0)),
        ],
        out_specs=pl.BlockSpec((None, tq, V_DIM), lambda b, h, i: (b, i, h)),
        out_shape=jax.ShapeDtypeStruct((batch, seq, N_HEADS * V_DIM), BF16),
        scratch_shapes=[
            pltpu.VMEM((2, ATTN_TILE, tk), F32), pltpu.VMEM((2, ATTN_TILE, LANES), F32),
            pltpu.VMEM((2, ATTN_TILE, LANES), F32), pltpu.VMEM((2, ATTN_TILE, LANES), F32),
            pltpu.VMEM((2, ATTN_TILE, V_DIM), F32),
        ],
        compiler_params=_params("parallel", "parallel", "arbitrary"),
        name="diff_attention",
    )(jnp.full((1,), lam_init, F32), lam_l, subln_g_l.reshape(1, V_DIM), qkv3, qkv3, qkv3, band)
    return out.reshape(batch * seq, N_HEADS * V_DIM)


FNET_S2 = 128
FNET_K1_GROUP = 8


def _angles(j, k, n):
    return ((j * k) % n).astype(F32) * (2.0 * math.pi / n)


def _chan_dft_table():
    n = FGROUP_DIM
    ang = _angles(lax.broadcasted_iota(jnp.int32, (n, n), 0), lax.broadcasted_iota(jnp.int32, (n, n), 1), n)
    return (jnp.concatenate([jnp.cos(ang), jnp.sin(ang)], axis=1) * n ** -0.5).astype(BF16)


def _seq_dft_tables(seq):
    s2 = min(FNET_S2, seq)
    s1 = seq // s2
    a1 = _angles(lax.broadcasted_iota(jnp.int32, (s1, s1), 0), lax.broadcasted_iota(jnp.int32, (s1, s1), 1), s1)
    c1, sn1 = jnp.cos(a1), jnp.sin(a1)
    m1 = jnp.concatenate([jnp.concatenate([c1, -sn1], axis=1), jnp.concatenate([-sn1, -c1], axis=1)], axis=0)
    k1 = lax.broadcasted_iota(jnp.int32, (s1, s2, s2), 0)
    k2 = lax.broadcasted_iota(jnp.int32, (s1, s2, s2), 1)
    n2 = lax.broadcasted_iota(jnp.int32, (s1, s2, s2), 2)
    a2 = _angles(k1 + s1 * k2, n2, seq)
    g = jnp.concatenate([jnp.cos(a2), jnp.sin(a2)], axis=2)
    return (m1 * s1 ** -0.5).astype(BF16), (g * s2 ** -0.5).astype(BF16)


def _chan_dft_kernel(f_ref, cs_ref, o_ref):
    cs = cs_ref[...]
    for g in range(N_FGROUPS):
        lo = g * FGROUP_DIM
        z = jnp.dot(f_ref[:, lo:lo + FGROUP_DIM], cs, preferred_element_type=F32)
        o_ref[0, :, lo:lo + FGROUP_DIM] = z[:, :FGROUP_DIM].astype(o_ref.dtype)
        o_ref[1, :, lo:lo + FGROUP_DIM] = z[:, FGROUP_DIM:].astype(o_ref.dtype)


def _seq_stage1_kernel(m_ref, z_ref, o_ref):
    o_ref[...] = jnp.dot(m_ref[...], z_ref[...], preferred_element_type=F32).astype(o_ref.dtype)


def _seq_stage2_kernel(g_ref, a_ref, o_ref, *, group, width):
    for j in range(group):
        a = a_ref[:, j].reshape(2 * g_ref.shape[1], width)
        o_ref[:, j * width:(j + 1) * width] = jnp.dot(g_ref[j], a, preferred_element_type=F32).astype(o_ref.dtype)


def _fourier_mix(qkvf, f_col_block, chan_cs, seq_tabs, batch, seq):
    fw = N_FGROUPS * FGROUP_DIM
    m1, g2 = seq_tabs
    s1, s2 = g2.shape[0], g2.shape[1]
    bm = _tile(seq, 1024)
    qkv3 = qkvf.reshape(batch, seq, qkvf.shape[-1])
    z = pl.pallas_call(
        _chan_dft_kernel,
        grid=(batch, seq // bm),
        in_specs=[
            pl.BlockSpec((None, bm, fw), lambda b, i: (b, i, f_col_block)),
            pl.BlockSpec((FGROUP_DIM, 2 * FGROUP_DIM), lambda b, i: (0, 0)),
        ],
        out_specs=pl.BlockSpec((None, 2, bm, fw), lambda b, i: (b, 0, i, 0)),
        out_shape=jax.ShapeDtypeStruct((batch, 2, seq, fw), BF16),
        compiler_params=_params("parallel", "parallel"),
        name="fnet_channel_dft",
    )(qkv3, chan_cs)

    ncol = s2 * fw
    bn = _tile(ncol, 8192)
    a = pl.pallas_call(
        _seq_stage1_kernel,
        grid=(batch, ncol // bn),
        in_specs=[
            pl.BlockSpec((2 * s1, 2 * s1), lambda b, j: (0, 0)),
            pl.BlockSpec((None, 2 * s1, bn), lambda b, j: (b, 0, j)),
        ],
        out_specs=pl.BlockSpec((None, 2 * s1, bn), lambda b, j: (b, 0, j)),
        out_shape=jax.ShapeDtypeStruct((batch, 2 * s1, ncol), BF16),
        compiler_params=_params("parallel", "parallel"),
        name="fnet_seq_dft_stage1",
    )(m1, z.reshape(batch, 2 * s1, ncol))

    gk = min(FNET_K1_GROUP, s1)
    assert s1 % gk == 0
    y = pl.pallas_call(
        functools.partial(_seq_stage2_kernel, group=gk, width=fw),
        grid=(batch, s1 // gk),
        in_specs=[
            pl.BlockSpec((gk, s2, 2 * s2), lambda b, j: (j, 0, 0)),
            pl.BlockSpec((None, 2, gk, s2, fw), lambda b, j: (b, 0, j, 0, 0)),
        ],
        out_specs=pl.BlockSpec((None, s2, gk * fw), lambda b, j: (b, 0, j)),
        out_shape=jax.ShapeDtypeStruct((batch, s2, s1 * fw), BF16),
        compiler_params=_params("parallel", "parallel"),
        name="fnet_seq_dft_stage2",
    )(g2, a.reshape(batch, 2, s1, s2, fw))
    return y.reshape(batch * seq, fw)


def _merge_kernel(o_ref, f_ref, wa_ref, wf_ref, g1_ref, g2_ref, out_ref):
    ya = jnp.dot(o_ref[...], wa_ref[...], preferred_element_type=F32)
    yf = jnp.dot(f_ref[...], wf_ref[...], preferred_element_type=F32)
    out_ref[...] = (g1_ref[...].astype(F32) * ya + g2_ref[...].astype(F32) * yf).astype(out_ref.dtype)


def _gated_merge(o, fm, gates, w_attn, w_fnet, layer):
    t, ka = o.shape
    kf = fm.shape[1]
    d = w_attn.shape[-1]
    bm = _tile(t, 1024)
    bn = _tile(d, 512)
    nb = d // bn
    return pl.pallas_call(
        _merge_kernel,
        grid=(nb, t // bm),
        in_specs=[
            pl.BlockSpec((bm, ka), lambda j, i: (i, 0)),
            pl.BlockSpec((bm, kf), lambda j, i: (i, 0)),
            pl.BlockSpec((None, ka, bn), lambda j, i: (layer, 0, j)),
            pl.BlockSpec((None, kf, bn), lambda j, i: (layer, 0, j)),
            pl.BlockSpec((bm, bn), lambda j, i: (i, j)),
            pl.BlockSpec((bm, bn), lambda j, i: (i, nb + j)),
        ],
        out_specs=pl.BlockSpec((bm, bn), lambda j, i: (i, j)),
        out_shape=jax.ShapeDtypeStruct((t, d), BF16),
        compiler_params=_params("parallel", "parallel"),
        name="gated_merge",
    )(o, fm, w_attn, w_fnet, gates, gates)


def _proj_res_ln_kernel(a_ref, w_ref, x_ref, g_ref, b_ref, of_ref, ob_ref, acc_ref, *, alpha):
    k = pl.program_id(1)

    @pl.when(k == 0)
    def _():
        acc_ref[...] = jnp.zeros_like(acc_ref)

    acc_ref[...] += jnp.dot(a_ref[...], w_ref[...], preferred_element_type=F32)

    @pl.when(k == pl.num_programs(1) - 1)
    def _():
        y = _layernorm_rows(alpha * x_ref[...] + acc_ref[...], g_ref[...], b_ref[...])
        of_ref[...] = y
        ob_ref[...] = y.astype(BF16)


def _proj_residual_layernorm(a, w, layer, xf, g, b, alpha, bk_pref):
    t, kdim = a.shape
    d = w.shape[-1]
    bm = _tile(t, 512)
    bk = _tile(kdim, bk_pref)
    row = pl.BlockSpec((bm, d), lambda i, k: (i, 0))
    vec = pl.BlockSpec((None, 1, d), lambda i, k: (layer, 0, 0))
    return pl.pallas_call(
        functools.partial(_proj_res_ln_kernel, alpha=alpha),
        grid=(t // bm, kdim // bk),
        in_specs=[
            pl.BlockSpec((bm, bk), lambda i, k: (i, k)),
            pl.BlockSpec((None, bk, d), lambda i, k: (layer, k, 0)),
            row, vec, vec,
        ],
        out_specs=[row, row],
        out_shape=[jax.ShapeDtypeStruct((t, d), F32), jax.ShapeDtypeStruct((t, d), BF16)],
        scratch_shapes=[pltpu.VMEM((bm, d), F32)],
        compiler_params=_params("parallel", "arbitrary"),
        name="proj_residual_layernorm",
    )(a, w, xf, g, b)


def _swiglu_kernel(x_ref, wg_ref, wu_ref, o_ref):
    x = x_ref[...]
    gate = jnp.dot(x, wg_ref[...], preferred_element_type=F32)
    up = jnp.dot(x, wu_ref[...], preferred_element_type=F32)
    o_ref[...] = (gate * jax.nn.sigmoid(gate) * up).astype(o_ref.dtype)


def _swiglu_up(xb, w_gu, layer, d_ff):
    t, d = xb.shape
    bm = _tile(t, 1024)
    bn = _tile(d_ff, 512)
    nb = d_ff // bn
    return pl.pallas_call(
        _swiglu_kernel,
        grid=(nb, t // bm),
        in_specs=[
            pl.BlockSpec((bm, d), lambda j, i: (i, 0)),
            pl.BlockSpec((None, d, bn), lambda j, i: (layer, 0, j)),
            pl.BlockSpec((None, d, bn), lambda j, i: (layer, 0, nb + j)),
        ],
        out_specs=pl.BlockSpec((bm, bn), lambda j, i: (i, j)),
        out_shape=jax.ShapeDtypeStruct((t, d_ff), BF16),
        compiler_params=_params("parallel", "parallel"),
        name="swiglu_up",
    )(xb, w_gu, w_gu)


def _trunk(x, band, chan_cs, seq_tabs, p):
    batch, seq, d = x.shape
    depth = p["w_in"].shape[0]
    alpha = (2.0 * depth) ** 0.25
    qk_width = N_HEADS * 2 * HEAD_DIM
    attn_width = N_HEADS * V_DIM
    fnet_width = N_FGROUPS * FGROUP_DIM
    n_main = 2 * qk_width + attn_width + fnet_width
    assert (2 * qk_width + attn_width) % fnet_width == 0
    f_col_block = (2 * qk_width + attn_width) // fnet_width
    d_ff = p["w_down"].shape[1]

    col_scale = jnp.concatenate(
        [jnp.full((qk_width,), HEAD_DIM ** -0.5 * LOG2_E, F32), jnp.ones((n_main - qk_width,), F32)]).reshape(1, n_main)

    xf, xb = _input_layernorm(x.reshape(batch * seq, d), p["ln_in_g"], p["ln_in_b"])
    for l in range(depth):
        lam_init = 0.8 - 0.6 * math.exp(-0.3 * l)
        qkvf, gates = _in_projection(xb, p["w_in"], l, col_scale, p["b_gate"], n_main)
        o = _diff_attention(qkvf, p["lam"][l], p["subln_g"][l], band, batch, seq, lam_init)
        fm = _fourier_mix(qkvf, f_col_block, chan_cs, seq_tabs, batch, seq)
        merged = _gated_merge(o, fm, gates, p["w_br_attn"], p["w_br_fnet"], l)
        xf, xb = _proj_residual_layernorm(merged, p["w_out"], l, xf, p["ln1_g"], p["ln1_b"], alpha, 2048)
        act = _swiglu_up(xb, p["w_gu"], l, d_ff)
        xf, xb = _proj_residual_layernorm(act, p["w_down"], l, xf, p["ln2_g"], p["ln2_b"], alpha, 1408)
    return xf.reshape(batch, seq, d)


def kernel(x_prompt, x_sample, rel_bias, ln_in_g, ln_in_b, w_in, b_gate, lam, subln_g, w_br_attn, w_br_fnet,
           w_out, ln1_g, ln1_b, w_gu, w_down, ln2_g, ln2_b):
    depth, d = ln1_g.shape
    p = {
        "ln_in_g": ln_in_g, "ln_in_b": ln_in_b,
        "w_in": w_in.astype(BF16), "b_gate": b_gate.reshape(depth, 1, -1),
        "lam": lam, "subln_g": subln_g,
        "w_br_attn": w_br_attn.astype(BF16), "w_br_fnet": w_br_fnet.astype(BF16),
        "w_out": w_out.astype(BF16), "w_gu": w_gu.astype(BF16), "w_down": w_down.astype(BF16),
        "ln1_g": ln1_g.reshape(depth, 1, d), "ln1_b": ln1_b.reshape(depth, 1, d),
        "ln2_g": ln2_g.reshape(depth, 1, d), "ln2_b": ln2_b.reshape(depth, 1, d),
    }
    band = _bias_tiles(rel_bias)
    chan_cs = _chan_dft_table()
    return tuple(_trunk(x, band, chan_cs, _seq_dft_tables(x.shape[1]), p) for x in (x_prompt, x_sample))
```

```python
import functools
import math

import jax
import jax.numpy as jnp
from jax import lax
from jax.experimental import pallas as pl
from jax.experimental.pallas import tpu as pltpu

N_HEADS = 8
HEAD_DIM = 128
V_DIM = 2 * HEAD_DIM
N_FGROUPS = 8
FGROUP_DIM = 128
N_BUCKETS = 32
MAX_DISTANCE = 128
LN_EPS = 1e-5
LOG2_E = math.log2(math.e)

LANES = 128
ATTN_TILE = 256
ATTN_KV_CHUNK = 8192
ATTN_ROWS_PER_STEP = 2048
VMEM_LIMIT_BYTES = 56 * 1024 * 1024

F32 = jnp.float32
BF16 = jnp.bfloat16


def _tile(n, pref):
    if n <= pref:
        return n
    t = (pref // LANES) * LANES
    while t >= LANES:
        if n % t == 0:
            return t
        t -= LANES
    raise ValueError(f"no lane-aligned tile for {n}")


def _params(*sem):
    return pltpu.CompilerParams(dimension_semantics=sem, vmem_limit_bytes=VMEM_LIMIT_BYTES)


def _layernorm_rows(y, g, b):
    mu = jnp.mean(y, axis=-1, keepdims=True)
    yc = y - mu
    var = jnp.mean(yc * yc, axis=-1, keepdims=True)
    return yc * lax.rsqrt(var + LN_EPS) * g + b


def _ln_kernel(x_ref, g_ref, b_ref, of_ref, ob_ref):
    y = _layernorm_rows(x_ref[...], g_ref[...], b_ref[...])
    of_ref[...] = y
    ob_ref[...] = y.astype(BF16)


def _input_layernorm(x, g, b):
    t, d = x.shape
    bm = _tile(t, 512)
    row = pl.BlockSpec((bm, d), lambda i: (i, 0))
    vec = pl.BlockSpec((1, d), lambda i: (0, 0))
    return pl.pallas_call(
        _ln_kernel,
        grid=(t // bm,),
        in_specs=[row, vec, vec],
        out_specs=[row, row],
        out_shape=[jax.ShapeDtypeStruct((t, d), F32), jax.ShapeDtypeStruct((t, d), BF16)],
        compiler_params=_params("parallel"),
        name="input_layernorm",
    )(x, g.reshape(1, d), b.reshape(1, d))


def _proj_kernel(x_ref, w_ref, s_ref, o_ref):
    acc = jnp.dot(x_ref[...], w_ref[...], preferred_element_type=F32)
    o_ref[...] = (acc * s_ref[...]).astype(o_ref.dtype)


def _gate_kernel(x_ref, w_ref, b_ref, o_ref):
    acc = jnp.dot(x_ref[...], w_ref[...], preferred_element_type=F32)
    o_ref[...] = jax.nn.sigmoid(acc + b_ref[...]).astype(o_ref.dtype)


def _in_projection(xb, w_in, layer, col_scale, b_gate, n_main):
    t, d = xb.shape
    n_gate = w_in.shape[-1] - n_main
    bm = _tile(t, 1024)

    bn = _tile(n_main, 1024)
    qkvf = pl.pallas_call(
        _proj_kernel,
        grid=(n_main // bn, t // bm),
        in_specs=[
            pl.BlockSpec((bm, d), lambda j, i: (i, 0)),
            pl.BlockSpec((None, d, bn), lambda j, i: (layer, 0, j)),
            pl.BlockSpec((1, bn), lambda j, i: (0, j)),
        ],
        out_specs=pl.BlockSpec((bm, bn), lambda j, i: (i, j)),
        out_shape=jax.ShapeDtypeStruct((t, n_main), BF16),
        compiler_params=_params("parallel", "parallel"),
        name="in_proj_qkvf",
    )(xb, w_in, col_scale)

    bg = _tile(n_gate, 1024)
    assert n_main % bg == 0
    off = n_main // bg
    gates = pl.pallas_call(
        _gate_kernel,
        grid=(n_gate // bg, t // bm),
        in_specs=[
            pl.BlockSpec((bm, d), lambda j, i: (i, 0)),
            pl.BlockSpec((None, d, bg), lambda j, i: (layer, 0, off + j)),
            pl.BlockSpec((None, 1, bg), lambda j, i: (layer, 0, j)),
        ],
        out_specs=pl.BlockSpec((bm, bg), lambda j, i: (i, j)),
        out_shape=jax.ShapeDtypeStruct((t, n_gate), BF16),
        compiler_params=_params("parallel", "parallel"),
        name="in_proj_gates",
    )(xb, w_in, b_gate)
    return qkvf, gates


def _t5_bucket(rel):
    nb = N_BUCKETS // 2
    ret = jnp.where(rel > 0, nb, 0)
    n = jnp.abs(rel)
    max_exact = nb // 2
    nf = jnp.maximum(n, 1).astype(F32)
    large = max_exact + (jnp.log(nf / max_exact) / math.log(MAX_DISTANCE / max_exact) * (nb - max_exact)).astype(jnp.int32)
    large = jnp.minimum(large, nb - 1)
    return ret + jnp.where(n < max_exact, n, large)


N_BIAS_TILES = 5


def _bias_tile_kernel(tab_ref, ids_ref, o_ref):
    h = pl.program_id(0)
    ids = ids_ref[...]
    acc = jnp.zeros(ids.shape, F32)
    for b in range(N_BUCKETS):
        acc = jnp.where(ids == b, tab_ref[b, h], acc)
    o_ref[...] = acc * LOG2_E


def _bias_tiles(rel_bias):
    tb = ATTN_TILE
    assert tb >= MAX_DISTANCE
    i = lax.broadcasted_iota(jnp.int32, (N_BIAS_TILES, tb, tb), 1)
    j = lax.broadcasted_iota(jnp.int32, (N_BIAS_TILES, tb, tb), 2)
    d = lax.broadcasted_iota(jnp.int32, (N_BIAS_TILES, tb, tb), 0) - 2
    ids = _t5_bucket(d * tb + j - i).reshape(N_BIAS_TILES * tb, tb)
    out = pl.pallas_call(
        _bias_tile_kernel,
        grid=(N_HEADS,),
        in_specs=[
            pl.BlockSpec(memory_space=pltpu.SMEM),
            pl.BlockSpec((N_BIAS_TILES * tb, tb), lambda h: (0, 0)),
        ],
        out_specs=pl.BlockSpec((None, N_BIAS_TILES * tb, tb), lambda h: (h, 0, 0)),
        out_shape=jax.ShapeDtypeStruct((N_HEADS, N_BIAS_TILES * tb, tb), F32),
        compiler_params=_params("parallel"),
        name="t5_bias_tiles",
    )(rel_bias, ids)
    return out.reshape(N_HEADS, N_BIAS_TILES, tb, tb)


def _lane_wide(x, width):
    return jnp.concatenate([x] * (width // LANES), axis=1)


def _attn_kernel(lam_init_ref, lam_ref, g_ref, q_ref, k_ref, v_ref, band_ref, o_ref,
                 s_ref, mx_ref, m_ref, l_ref, acc_ref, *, n_blocks, n_chunks, tk):
    tq = ATTN_TILE
    sub = tk // ATTN_TILE
    n_items = n_blocks * n_chunks
    assert n_items >= 2
    group = pl.program_id(2)
    nt = (((1,), (1,)), ((), ()))

    lam_init = lam_init_ref[0]
    lam = lam_ref[...]
    lam_full = (jnp.exp(jnp.sum(lam[0:1] * lam[1:2], axis=-1, keepdims=True))
                - jnp.exp(jnp.sum(lam[2:3] * lam[3:4], axis=-1, keepdims=True)) + lam_init)
    gain = g_ref[...] * (1.0 - lam_init)

    if n_chunks > 1:
        m_ref[...] = jnp.zeros(m_ref.shape, F32)
        l_ref[...] = jnp.zeros(l_ref.shape, F32)
        acc_ref[...] = jnp.zeros(acc_ref.shape, F32)

    def position(t):
        if isinstance(t, int):
            return t // n_chunks, t % n_chunks
        return lax.div(t, n_chunks), lax.rem(t, n_chunks)

    def rows_of(i):
        return pl.ds(i * tq, tq) if isinstance(i, int) else pl.ds(pl.multiple_of(i * tq, tq), tq)

    def keys_of(c, u):
        start = c * tk + u * ATTN_TILE
        return pl.ds(start if isinstance(c, int) else pl.multiple_of(start, ATTN_TILE), ATTN_TILE)

    def half_max(x):
        return jnp.maximum(x[:, :LANES], x[:, LANES:])

    def half_sum(x):
        return x[:, :LANES] + x[:, LANES:]

    def finish_rows(i, o1, o2):
        of = o1 - lam_full * o2
        of = of * lax.rsqrt(jnp.mean(of * of, axis=-1, keepdims=True) + LN_EPS)
        o_ref[rows_of(i), :] = (of * gain).astype(o_ref.dtype)

    def run(a_item, b_item):
        if a_item is not None:
            a_i, a_c = position(a_item)
            a_row = group * n_blocks + a_i
            a_q = [q_ref[rows_of(a_i), mi * HEAD_DIM:(mi + 1) * HEAD_DIM] for mi in range(2)]
            a_max = [None, None]
        if b_item is not None:
            b_i, b_c = position(b_item)
            if n_chunks == 1:
                b_m = [mx_ref[mi] for mi in range(2)]
            else:
                b_m_prev = [jnp.where(b_c == 0, -jnp.inf, m_ref[mi]) for mi in range(2)]
                b_m = [jnp.maximum(b_m_prev[mi], mx_ref[mi]) for mi in range(2)]
                b_alpha = [jnp.exp2(b_m_prev[mi] - b_m[mi]) for mi in range(2)]
            b_m_wide = [_lane_wide(b_m[mi], ATTN_TILE) for mi in range(2)]
            b_sum = [None, None]
            b_acc = [None, None]

        for u in range(sub):
            cols = slice(u * ATTN_TILE, (u + 1) * ATTN_TILE)
            if b_item is not None:
                vu = v_ref[keys_of(b_c, u), :]
                for mi in range(2):
                    p = jnp.exp2(s_ref[mi, :, cols] - b_m_wide[mi])
                    b_sum[mi] = half_sum(p) if b_sum[mi] is None else b_sum[mi] + half_sum(p)
                    pv = jnp.dot(p.astype(BF16), vu, preferred_element_type=F32)
                    b_acc[mi] = pv if b_acc[mi] is None else b_acc[mi] + pv
            if a_item is not None:
                tile = band_ref[jnp.clip(a_c * sub + u - a_row, -2, 2) + 2]
                for mi in range(2):
                    s = lax.dot_general(a_q[mi], k_ref[keys_of(a_c, u), mi * HEAD_DIM:(mi + 1) * HEAD_DIM],
                                        nt, preferred_element_type=F32) + tile
                    s_ref[mi, :, cols] = s
                    a_max[mi] = half_max(s) if a_max[mi] is None else jnp.maximum(a_max[mi], half_max(s))

        if a_item is not None:
            for mi in range(2):
                mx_ref[mi] = jnp.broadcast_to(jnp.max(a_max[mi], axis=-1, keepdims=True), (tq, LANES))
        if b_item is not None:
            b_l = [jnp.sum(b_sum[mi], axis=-1, keepdims=True) for mi in range(2)]
            if n_chunks == 1:
                finish_rows(b_i, b_acc[0] / b_l[0], b_acc[1] / b_l[1])
            else:
                for mi in range(2):
                    l_ref[mi] = b_alpha[mi] * l_ref[mi] + b_l[mi]
                    m_ref[mi] = b_m[mi]
                    acc_ref[mi] = _lane_wide(b_alpha[mi], V_DIM) * acc_ref[mi] + b_acc[mi]

                def last_chunk():
                    finish_rows(b_i, acc_ref[0] / _lane_wide(l_ref[0], V_DIM),
                                acc_ref[1] / _lane_wide(l_ref[1], V_DIM))
                if isinstance(b_c, int):
                    if b_c == n_chunks - 1:
                        last_chunk()
                else:
                    pl.when(b_c == n_chunks - 1)(last_chunk)

    run(0, None)

    def body(t, carry):
        run(t + 1, t)
        return carry

    lax.fori_loop(0, n_items - 1, body, 0)
    run(None, n_items - 1)


def _diff_attention(qkvf, lam_l, subln_g_l, band, batch, seq, lam_init):
    tq = _tile(seq, ATTN_ROWS_PER_STEP)
    assert tq % ATTN_TILE == 0 and seq % tq == 0
    tk = _tile(seq, ATTN_KV_CHUNK)
    assert tk % ATTN_TILE == 0
    n_chunks = seq // tk
    qkv3 = qkvf.reshape(batch, seq, qkvf.shape[-1])
    kern = functools.partial(_attn_kernel, n_blocks=tq // ATTN_TILE, n_chunks=n_chunks, tk=tk)
    out = pl.pallas_call(
        kern,
        grid=(batch, N_HEADS, seq // tq),
        in_specs=[
            pl.BlockSpec(memory_space=pltpu.SMEM),
            pl.BlockSpec((4, HEAD_DIM), lambda b, h, i: (0, 0)),
            pl.BlockSpec((1, V_DIM), lambda b, h, i: (0, 0)),
            pl.BlockSpec((None, tq, V_DIM), lambda b, h, i: (b, i, h)),
            pl.BlockSpec((None, seq, V_DIM), lambda b, h, i: (b, 0, N_HEADS + h)),
            pl.BlockSpec((None, seq, V_DIM), lambda b, h, i: (b, 0, 2 * N_HEADS + h)),
            pl.BlockSpec((None, N_BIAS_TILES, ATTN_TILE, ATTN_TILE), lambda b, h, i: (h, 0, 0, 0)),
        ],
        out_specs=pl.BlockSpec((None, tq, V_DIM), lambda b, h, i: (b, i, h)),
        out_shape=jax.ShapeDtypeStruct((batch, seq, N_HEADS * V_DIM), BF16),
        scratch_shapes=[
            pltpu.VMEM((2, ATTN_TILE, tk), F32), pltpu.VMEM((2, ATTN_TILE, LANES), F32),
            pltpu.VMEM((2, ATTN_TILE, LANES), F32), pltpu.VMEM((2, ATTN_TILE, LANES), F32),
            pltpu.VMEM((2, ATTN_TILE, V_DIM), F32),
        ],
        compiler_params=_params("parallel", "parallel", "arbitrary"),
        name="diff_attention",
    )(jnp.full((1,), lam_init, F32), lam_l, subln_g_l.reshape(1, V_DIM), qkv3, qkv3, qkv3, band)
    return out.reshape(batch * seq, N_HEADS * V_DIM)


FNET_S2 = 128
FNET_K1_GROUP = 8


def _angles(j, k, n):
    return ((j * k) % n).astype(F32) * (2.0 * math.pi / n)


def _chan_dft_table():
    n = FGROUP_DIM
    ang = _angles(lax.broadcasted_iota(jnp.int32, (n, n), 0), lax.broadcasted_iota(jnp.int32, (n, n), 1), n)
    return (jnp.concatenate([jnp.cos(ang), jnp.sin(ang)], axis=1) * n ** -0.5).astype(BF16)


def _seq_dft_tables(seq):
    s2 = min(FNET_S2, seq)
    s1 = seq // s2
    a1 = _angles(lax.broadcasted_iota(jnp.int32, (s1, s1), 0), lax.broadcasted_iota(jnp.int32, (s1, s1), 1), s1)
    c1, sn1 = jnp.cos(a1), jnp.sin(a1)
    m1 = jnp.concatenate([jnp.concatenate([c1, -sn1], axis=1), jnp.concatenate([-sn1, -c1], axis=1)], axis=0)
    k1 = lax.broadcasted_iota(jnp.int32, (s1, s2, s2), 0)
    k2 = lax.broadcasted_iota(jnp.int32, (s1, s2, s2), 1)
    n2 = lax.broadcasted_iota(jnp.int32, (s1, s2, s2), 2)
    a2 = _angles(k1 + s1 * k2, n2, seq)
    g = jnp.concatenate([jnp.cos(a2), jnp.sin(a2)], axis=2)
    return (m1 * s1 ** -0.5).astype(BF16), (g * s2 ** -0.5).astype(BF16)


def _chan_dft_kernel(f_ref, cs_ref, o_ref):
    cs = cs_ref[...]
    for g in range(N_FGROUPS):
        lo = g * FGROUP_DIM
        z = jnp.dot(f_ref[:, lo:lo + FGROUP_DIM], cs, preferred_element_type=F32)
        o_ref[0, :, lo:lo + FGROUP_DIM] = z[:, :FGROUP_DIM].astype(o_ref.dtype)
        o_ref[1, :, lo:lo + FGROUP_DIM] = z[:, FGROUP_DIM:].astype(o_ref.dtype)


def _seq_stage1_kernel(m_ref, z_ref, o_ref):
    o_ref[...] = jnp.dot(m_ref[...], z_ref[...], preferred_element_type=F32).astype(o_ref.dtype)


def _seq_stage2_kernel(g_ref, a_ref, o_ref, *, group, width):
    for j in range(group):
        a = a_ref[:, j].reshape(2 * g_ref.shape[1], width)
        o_ref[:, j * width:(j + 1) * width] = jnp.dot(g_ref[j], a, preferred_element_type=F32).astype(o_ref.dtype)


def _fourier_mix(qkvf, f_col_block, chan_cs, seq_tabs, batch, seq):
    fw = N_FGROUPS * FGROUP_DIM
    m1, g2 = seq_tabs
    s1, s2 = g2.shape[0], g2.shape[1]
    bm = _tile(seq, 1024)
    qkv3 = qkvf.reshape(batch, seq, qkvf.shape[-1])
    z = pl.pallas_call(
        _chan_dft_kernel,
        grid=(batch, seq // bm),
        in_specs=[
            pl.BlockSpec((None, bm, fw), lambda b, i: (b, i, f_col_block)),
            pl.BlockSpec((FGROUP_DIM, 2 * FGROUP_DIM), lambda b, i: (0, 0)),
        ],
        out_specs=pl.BlockSpec((None, 2, bm, fw), lambda b, i: (b, 0, i, 0)),
        out_shape=jax.ShapeDtypeStruct((batch, 2, seq, fw), BF16),
        compiler_params=_params("parallel", "parallel"),
        name="fnet_channel_dft",
    )(qkv3, chan_cs)

    ncol = s2 * fw
    bn = _tile(ncol, 8192)
    a = pl.pallas_call(
        _seq_stage1_kernel,
        grid=(batch, ncol // bn),
        in_specs=[
            pl.BlockSpec((2 * s1, 2 * s1), lambda b, j: (0, 0)),
            pl.BlockSpec((None, 2 * s1, bn), lambda b, j: (b, 0, j)),
        ],
        out_specs=pl.BlockSpec((None, 2 * s1, bn), lambda b, j: (b, 0, j)),
        out_shape=jax.ShapeDtypeStruct((batch, 2 * s1, ncol), BF16),
        compiler_params=_params("parallel", "parallel"),
        name="fnet_seq_dft_stage1",
    )(m1, z.reshape(batch, 2 * s1, ncol))

    gk = min(FNET_K1_GROUP, s1)
    assert s1 % gk == 0
    y = pl.pallas_call(
        functools.partial(_seq_stage2_kernel, group=gk, width=fw),
        grid=(batch, s1 // gk),
        in_specs=[
            pl.BlockSpec((gk, s2, 2 * s2), lambda b, j: (j, 0, 0)),
            pl.BlockSpec((None, 2, gk, s2, fw), lambda b, j: (b, 0, j, 0, 0)),
        ],
        out_specs=pl.BlockSpec((None, s2, gk * fw), lambda b, j: (b, 0, j)),
        out_shape=jax.ShapeDtypeStruct((batch, s2, s1 * fw), BF16),
        compiler_params=_params("parallel", "parallel"),
        name="fnet_seq_dft_stage2",
    )(g2, a.reshape(batch, 2, s1, s2, fw))
    return y.reshape(batch * seq, fw)


def _merge_kernel(o_ref, f_ref, wa_ref, wf_ref, g1_ref, g2_ref, out_ref):
    ya = jnp.dot(o_ref[...], wa_ref[...], preferred_element_type=F32)
    yf = jnp.dot(f_ref[...], wf_ref[...], preferred_element_type=F32)
    out_ref[...] = (g1_ref[...].astype(F32) * ya + g2_ref[...].astype(F32) * yf).astype(out_ref.dtype)


def _gated_merge(o, fm, gates, w_attn, w_fnet, layer):
    t, ka = o.shape
    kf = fm.shape[1]
    d = w_attn.shape[-1]
    bm = _tile(t, 1024)
    bn = _tile(d, 512)
    nb = d // bn
    return pl.pallas_call(
        _merge_kernel,
        grid=(nb, t // bm),
        in_specs=[
            pl.BlockSpec((bm, ka), lambda j, i: (i, 0)),
            pl.BlockSpec((bm, kf), lambda j, i: (i, 0)),
            pl.BlockSpec((None, ka, bn), lambda j, i: (layer, 0, j)),
            pl.BlockSpec((None, kf, bn), lambda j, i: (layer, 0, j)),
            pl.BlockSpec((bm, bn), lambda j, i: (i, j)),
            pl.BlockSpec((bm, bn), lambda j, i: (i, nb + j)),
        ],
        out_specs=pl.BlockSpec((bm, bn), lambda j, i: (i, j)),
        out_shape=jax.ShapeDtypeStruct((t, d), BF16),
        compiler_params=_params("parallel", "parallel"),
        name="gated_merge",
    )(o, fm, w_attn, w_fnet, gates, gates)


def _proj_res_ln_kernel(a_ref, w_ref, x_ref, g_ref, b_ref, of_ref, ob_ref, acc_ref, *, alpha):
    k = pl.program_id(1)

    @pl.when(k == 0)
    def _():
        acc_ref[...] = jnp.zeros_like(acc_ref)

    acc_ref[...] += jnp.dot(a_ref[...], w_ref[...], preferred_element_type=F32)

    @pl.when(k == pl.num_programs(1) - 1)
    def _():
        y = _layernorm_rows(alpha * x_ref[...] + acc_ref[...], g_ref[...], b_ref[...])
        of_ref[...] = y
        ob_ref[...] = y.astype(BF16)


def _proj_residual_layernorm(a, w, layer, xf, g, b, alpha, bk_pref):
    t, kdim = a.shape
    d = w.shape[-1]
    bm = _tile(t, 512)
    bk = _tile(kdim, bk_pref)
    row = pl.BlockSpec((bm, d), lambda i, k: (i, 0))
    vec = pl.BlockSpec((None, 1, d), lambda i, k: (layer, 0, 0))
    return pl.pallas_call(
        functools.partial(_proj_res_ln_kernel, alpha=alpha),
        grid=(t // bm, kdim // bk),
        in_specs=[
            pl.BlockSpec((bm, bk), lambda i, k: (i, k)),
            pl.BlockSpec((None, bk, d), lambda i, k: (layer, k, 0)),
            row, vec, vec,
        ],
        out_specs=[row, row],
        out_shape=[jax.ShapeDtypeStruct((t, d), F32), jax.ShapeDtypeStruct((t, d), BF16)],
        scratch_shapes=[pltpu.VMEM((bm, d), F32)],
        compiler_params=_params("parallel", "arbitrary"),
        name="proj_residual_layernorm",
    )(a, w, xf, g, b)


def _swiglu_kernel(x_ref, wg_ref, wu_ref, o_ref):
    x = x_ref[...]
    gate = jnp.dot(x, wg_ref[...], preferred_element_type=F32)
    up = jnp.dot(x, wu_ref[...], preferred_element_type=F32)
    o_ref[...] = (gate * jax.nn.sigmoid(gate) * up).astype(o_ref.dtype)


def _swiglu_up(xb, w_gu, layer, d_ff):
    t, d = xb.shape
    bm = _tile(t, 1024)
    bn = _tile(d_ff, 512)
    nb = d_ff // bn
    return pl.pallas_call(
        _swiglu_kernel,
        grid=(nb, t // bm),
        in_specs=[
            pl.BlockSpec((bm, d), lambda j, i: (i, 0)),
            pl.BlockSpec((None, d, bn), lambda j, i: (layer, 0, j)),
            pl.BlockSpec((None, d, bn), lambda j, i: (layer, 0, nb + j)),
        ],
        out_specs=pl.BlockSpec((bm, bn), lambda j, i: (i, j)),
        out_shape=jax.ShapeDtypeStruct((t, d_ff), BF16),
        compiler_params=_params("parallel", "parallel"),
        name="swiglu_up",
    )(xb, w_gu, w_gu)


def _trunk(x, band, chan_cs, seq_tabs, p):
    batch, seq, d = x.shape
    depth = p["w_in"].shape[0]
    alpha = (2.0 * depth) ** 0.25
    qk_width = N_HEADS * 2 * HEAD_DIM
    attn_width = N_HEADS * V_DIM
    fnet_width = N_FGROUPS * FGROUP_DIM
    n_main = 2 * qk_width + attn_width + fnet_width
    assert (2 * qk_width + attn_width) % fnet_width == 0
    f_col_block = (2 * qk_width + attn_width) // fnet_width
    d_ff = p["w_down"].shape[1]

    col_scale = jnp.concatenate(
        [jnp.full((qk_width,), HEAD_DIM ** -0.5 * LOG2_E, F32), jnp.ones((n_main - qk_width,), F32)]).reshape(1, n_main)

    xf, xb = _input_layernorm(x.reshape(batch * seq, d), p["ln_in_g"], p["ln_in_b"])
    for l in range(depth):
        lam_init = 0.8 - 0.6 * math.exp(-0.3 * l)
        qkvf, gates = _in_projection(xb, p["w_in"], l, col_scale, p["b_gate"], n_main)
        o = _diff_attention(qkvf, p["lam"][l], p["subln_g"][l], band, batch, seq, lam_init)
        fm = _fourier_mix(qkvf, f_col_block, chan_cs, seq_tabs, batch, seq)
        merged = _gated_merge(o, fm, gates, p["w_br_attn"], p["w_br_fnet"], l)
        xf, xb = _proj_residual_layernorm(merged, p["w_out"], l, xf, p["ln1_g"], p["ln1_b"], alpha, 2048)
        act = _swiglu_up(xb, p["w_gu"], l, d_ff)
        xf, xb = _proj_residual_layernorm(act, p["w_down"], l, xf, p["ln2_g"], p["ln2_b"], alpha, 2816)
    return xf.reshape(batch, seq, d)


def kernel(x_prompt, x_sample, rel_bias, ln_in_g, ln_in_b, w_in, b_gate, lam, subln_g, w_br_attn, w_br_fnet,
           w_out, ln1_g, ln1_b, w_gu, w_down, ln2_g, ln2_b):
    depth, d = ln1_g.shape
    p = {
        "ln_in_g": ln_in_g, "ln_in_b": ln_in_b,
        "w_in": w_in.astype(BF16), "b_gate": b_gate.reshape(depth, 1, -1),
        "lam": lam, "subln_g": subln_g,
        "w_br_attn": w_br_attn.astype(BF16), "w_br_fnet": w_br_fnet.astype(BF16),
        "w_out": w_out.astype(BF16), "w_gu": w_gu.astype(BF16), "w_down": w_down.astype(BF16),
        "ln1_g": ln1_g.reshape(depth, 1, d), "ln1_b": ln1_b.reshape(depth, 1, d),
        "ln2_g": ln2_g.reshape(depth, 1, d), "ln2_b": ln2_b.reshape(depth, 1, d),
    }
    band = _bias_tiles(rel_bias)
    chan_cs = _chan_dft_table()
    return tuple(_trunk(x, band, chan_cs, _seq_dft_tables(x.shape[1]), p) for x in (x_prompt, x_sample))
```

```python
import functools
import math

import jax
import jax.numpy as jnp
from jax import lax
from jax.experimental import pallas as pl
from jax.experimental.pallas import tpu as pltpu

N_HEADS = 8
HEAD_DIM = 128
V_DIM = 2 * HEAD_DIM
N_FGROUPS = 8
FGROUP_DIM = 128
N_BUCKETS = 32
MAX_DISTANCE = 128
LN_EPS = 1e-5
LOG2_E = math.log2(math.e)

LANES = 128
ATTN_TILE = 256
ATTN_KV_CHUNK = 8192
ATTN_ROWS_PER_STEP = 2048
VMEM_LIMIT_BYTES = 56 * 1024 * 1024

F32 = jnp.float32
BF16 = jnp.bfloat16


def _tile(n, pref):
    if n <= pref:
        return n
    t = (pref // LANES) * LANES
    while t >= LANES:
        if n % t == 0:
            return t
        t -= LANES
    raise ValueError(f"no lane-aligned tile for {n}")


def _params(*sem):
    return pltpu.CompilerParams(dimension_semantics=sem, vmem_limit_bytes=VMEM_LIMIT_BYTES)


def _layernorm_rows(y, g, b):
    mu = jnp.mean(y, axis=-1, keepdims=True)
    yc = y - mu
    var = jnp.mean(yc * yc, axis=-1, keepdims=True)
    return yc * lax.rsqrt(var + LN_EPS) * g + b


def _ln_kernel(x_ref, g_ref, b_ref, of_ref, ob_ref):
    y = _layernorm_rows(x_ref[...], g_ref[...], b_ref[...])
    of_ref[...] = y
    ob_ref[...] = y.astype(BF16)


def _input_layernorm(x, g, b):
    t, d = x.shape
    bm = _tile(t, 512)
    row = pl.BlockSpec((bm, d), lambda i: (i, 0))
    vec = pl.BlockSpec((1, d), lambda i: (0, 0))
    return pl.pallas_call(
        _ln_kernel,
        grid=(t // bm,),
        in_specs=[row, vec, vec],
        out_specs=[row, row],
        out_shape=[jax.ShapeDtypeStruct((t, d), F32), jax.ShapeDtypeStruct((t, d), BF16)],
        compiler_params=_params("parallel"),
        name="input_layernorm",
    )(x, g.reshape(1, d), b.reshape(1, d))


def _proj_kernel(x_ref, w_ref, s_ref, o_ref):
    acc = jnp.dot(x_ref[...], w_ref[...], preferred_element_type=F32)
    o_ref[...] = (acc * s_ref[...]).astype(o_ref.dtype)


def _gate_kernel(x_ref, w_ref, b_ref, o_ref):
    acc = jnp.dot(x_ref[...], w_ref[...], preferred_element_type=F32)
    o_ref[...] = jax.nn.sigmoid(acc + b_ref[...]).astype(o_ref.dtype)


def _in_projection(xb, w_in, layer, col_scale, b_gate, n_main):
    t, d = xb.shape
    n_gate = w_in.shape[-1] - n_main
    bm = _tile(t, 1024)

    bn = _tile(n_main, 1024)
    qkvf = pl.pallas_call(
        _proj_kernel,
        grid=(n_main // bn, t // bm),
        in_specs=[
            pl.BlockSpec((bm, d), lambda j, i: (i, 0)),
            pl.BlockSpec((None, d, bn), lambda j, i: (layer, 0, j)),
            pl.BlockSpec((1, bn), lambda j, i: (0, j)),
        ],
        out_specs=pl.BlockSpec((bm, bn), lambda j, i: (i, j)),
        out_shape=jax.ShapeDtypeStruct((t, n_main), BF16),
        compiler_params=_params("parallel", "parallel"),
        name="in_proj_qkvf",
    )(xb, w_in, col_scale)

    bg = _tile(n_gate, 1024)
    assert n_main % bg == 0
    off = n_main // bg
    gates = pl.pallas_call(
        _gate_kernel,
        grid=(n_gate // bg, t // bm),
        in_specs=[
            pl.BlockSpec((bm, d), lambda j, i: (i, 0)),
            pl.BlockSpec((None, d, bg), lambda j, i: (layer, 0, off + j)),
            pl.BlockSpec((None, 1, bg), lambda j, i: (layer, 0, j)),
        ],
        out_specs=pl.BlockSpec((bm, bg), lambda j, i: (i, j)),
        out_shape=jax.ShapeDtypeStruct((t, n_gate), BF16),
        compiler_params=_params("parallel", "parallel"),
        name="in_proj_gates",
    )(xb, w_in, b_gate)
    return qkvf, gates


def _t5_bucket(rel):
    nb = N_BUCKETS // 2
    ret = jnp.where(rel > 0, nb, 0)
    n = jnp.abs(rel)
    max_exact = nb // 2
    nf = jnp.maximum(n, 1).astype(F32)
    large = max_exact + (jnp.log(nf / max_exact) / math.log(MAX_DISTANCE / max_exact) * (nb - max_exact)).astype(jnp.int32)
    large = jnp.minimum(large, nb - 1)
    return ret + jnp.where(n < max_exact, n, large)


N_BIAS_TILES = 5


def _bias_tile_kernel(tab_ref, ids_ref, o_ref):
    h = pl.program_id(0)
    ids = ids_ref[...]
    acc = jnp.zeros(ids.shape, F32)
    for b in range(N_BUCKETS):
        acc = jnp.where(ids == b, tab_ref[b, h], acc)
    o_ref[...] = acc * LOG2_E


def _bias_tiles(rel_bias):
    tb = ATTN_TILE
    assert tb >= MAX_DISTANCE
    i = lax.broadcasted_iota(jnp.int32, (N_BIAS_TILES, tb, tb), 1)
    j = lax.broadcasted_iota(jnp.int32, (N_BIAS_TILES, tb, tb), 2)
    d = lax.broadcasted_iota(jnp.int32, (N_BIAS_TILES, tb, tb), 0) - 2
    ids = _t5_bucket(d * tb + j - i).reshape(N_BIAS_TILES * tb, tb)
    out = pl.pallas_call(
        _bias_tile_kernel,
        grid=(N_HEADS,),
        in_specs=[
            pl.BlockSpec(memory_space=pltpu.SMEM),
            pl.BlockSpec((N_BIAS_TILES * tb, tb), lambda h: (0, 0)),
        ],
        out_specs=pl.BlockSpec((None, N_BIAS_TILES * tb, tb), lambda h: (h, 0, 0)),
        out_shape=jax.ShapeDtypeStruct((N_HEADS, N_BIAS_TILES * tb, tb), F32),
        compiler_params=_params("parallel"),
        name="t5_bias_tiles",
    )(rel_bias, ids)
    return out.reshape(N_HEADS, N_BIAS_TILES, tb, tb)


def _lane_wide(x, width):
    return jnp.concatenate([x] * (width // LANES), axis=1)


def _attn_kernel(lam_init_ref, lam_ref, g_ref, q_ref, k_ref, v_ref, band_ref, o_ref,
                 s_ref, mx_ref, m_ref, l_ref, acc_ref, *, n_blocks, n_chunks, tk):
    tq = ATTN_TILE
    sub = tk // ATTN_TILE
    n_items = n_blocks * n_chunks
    assert n_items >= 2
    group = pl.program_id(2)
    nt = (((1,), (1,)), ((), ()))

    lam_init = lam_init_ref[0]
    lam = lam_ref[...]
    lam_full = (jnp.exp(jnp.sum(lam[0:1] * lam[1:2], axis=-1, keepdims=True))
                - jnp.exp(jnp.sum(lam[2:3] * lam[3:4], axis=-1, keepdims=True)) + lam_init)
    gain = g_ref[...] * (1.0 - lam_init)

    if n_chunks > 1:
        m_ref[...] = jnp.zeros(m_ref.shape, F32)
        l_ref[...] = jnp.zeros(l_ref.shape, F32)
        acc_ref[...] = jnp.zeros(acc_ref.shape, F32)

    def position(t):
        if isinstance(t, int):
            return t // n_chunks, t % n_chunks
        return lax.div(t, n_chunks), lax.rem(t, n_chunks)

    def rows_of(i):
        return pl.ds(i * tq, tq) if isinstance(i, int) else pl.ds(pl.multiple_of(i * tq, tq), tq)

    def keys_of(c, u):
        start = c * tk + u * ATTN_TILE
        return pl.ds(start if isinstance(c, int) else pl.multiple_of(start, ATTN_TILE), ATTN_TILE)

    def half_max(x):
        return jnp.maximum(x[:, :LANES], x[:, LANES:])

    def half_sum(x):
        return x[:, :LANES] + x[:, LANES:]

    def finish_rows(i, o1, o2):
        of = o1 - lam_full * o2
        of = of * lax.rsqrt(jnp.mean(of * of, axis=-1, keepdims=True) + LN_EPS)
        o_ref[rows_of(i), :] = (of * gain).astype(o_ref.dtype)

    def run(a_item, b_item):
        if a_item is not None:
            a_i, a_c = position(a_item)
            a_row = group * n_blocks + a_i
            a_q = [q_ref[rows_of(a_i), mi * HEAD_DIM:(mi + 1) * HEAD_DIM] for mi in range(2)]
            a_max = [None, None]
        if b_item is not None:
            b_i, b_c = position(b_item)
            if n_chunks == 1:
                b_m = [mx_ref[mi] for mi in range(2)]
            else:
                b_m_prev = [jnp.where(b_c == 0, -jnp.inf, m_ref[mi]) for mi in range(2)]
                b_m = [jnp.maximum(b_m_prev[mi], mx_ref[mi]) for mi in range(2)]
                b_alpha = [jnp.exp2(b_m_prev[mi] - b_m[mi]) for mi in range(2)]
            b_m_wide = [_lane_wide(b_m[mi], ATTN_TILE) for mi in range(2)]
            b_sum = [None, None]
            b_acc = [None, None]

        for u in range(sub):
            cols = slice(u * ATTN_TILE, (u + 1) * ATTN_TILE)
            if b_item is not None:
                vu = v_ref[keys_of(b_c, u), :]
                for mi in range(2):
                    p = jnp.exp2(s_ref[mi, :, cols] - b_m_wide[mi])
                    b_sum[mi] = half_sum(p) if b_sum[mi] is None else b_sum[mi] + half_sum(p)
                    pv = jnp.dot(p.astype(BF16), vu, preferred_element_type=F32)
                    b_acc[mi] = pv if b_acc[mi] is None else b_acc[mi] + pv
            if a_item is not None:
                tile = band_ref[jnp.clip(a_c * sub + u - a_row, -2, 2) + 2]
                for mi in range(2):
                    s = lax.dot_general(a_q[mi], k_ref[keys_of(a_c, u), mi * HEAD_DIM:(mi + 1) * HEAD_DIM],
                                        nt, preferred_element_type=F32) + tile
                    s_ref[mi, :, cols] = s
                    a_max[mi] = half_max(s) if a_max[mi] is None else jnp.maximum(a_max[mi], half_max(s))

        if a_item is not None:
            for mi in range(2):
                mx_ref[mi] = jnp.broadcast_to(jnp.max(a_max[mi], axis=-1, keepdims=True), (tq, LANES))
        if b_item is not None:
            b_l = [jnp.sum(b_sum[mi], axis=-1, keepdims=True) for mi in range(2)]
            if n_chunks == 1:
                finish_rows(b_i, b_acc[0] / b_l[0], b_acc[1] / b_l[1])
            else:
                for mi in range(2):
                    l_ref[mi] = b_alpha[mi] * l_ref[mi] + b_l[mi]
                    m_ref[mi] = b_m[mi]
                    acc_ref[mi] = _lane_wide(b_alpha[mi], V_DIM) * acc_ref[mi] + b_acc[mi]

                def last_chunk():
                    finish_rows(b_i, acc_ref[0] / _lane_wide(l_ref[0], V_DIM),
                                acc_ref[1] / _lane_wide(l_ref[1], V_DIM))
                if isinstance(b_c, int):
                    if b_c == n_chunks - 1:
                        last_chunk()
                else:
                    pl.when(b_c == n_chunks - 1)(last_chunk)

    run(0, None)

    def body(t, carry):
        run(t + 1, t)
        return carry

    lax.fori_loop(0, n_items - 1, body, 0)
    run(None, n_items - 1)


def _diff_attention(qkvf, lam_l, subln_g_l, band, batch, seq, lam_init):
    tq = _tile(seq, ATTN_ROWS_PER_STEP)
    assert tq % ATTN_TILE == 0 and seq % tq == 0
    tk = _tile(seq, ATTN_KV_CHUNK)
    assert tk % ATTN_TILE == 0
    n_chunks = seq // tk
    qkv3 = qkvf.reshape(batch, seq, qkvf.shape[-1])
    kern = functools.partial(_attn_kernel, n_blocks=tq // ATTN_TILE, n_chunks=n_chunks, tk=tk)
    out = pl.pallas_call(
        kern,
        grid=(batch, N_HEADS, seq // tq),
        in_specs=[
            pl.BlockSpec(memory_space=pltpu.SMEM),
            pl.BlockSpec((4, HEAD_DIM), lambda b, h, i: (0, 0)),
            pl.BlockSpec((1, V_DIM), lambda b, h, i: (0, 0)),
            pl.BlockSpec((None, tq, V_DIM), lambda b, h, i: (b, i, h)),
            pl.BlockSpec((None, seq, V_DIM), lambda b, h, i: (b, 0, N_HEADS + h)),
            pl.BlockSpec((None, seq, V_DIM), lambda b, h, i: (b, 0, 2 * N_HEADS + h)),
            pl.BlockSpec((None, N_BIAS_TILES, ATTN_TILE, ATTN_TILE), lambda b, h, i: (h, 0, 0, 0)),
        ],
        out_specs=pl.BlockSpec((None, tq, V_DIM), lambda b, h, i: (b, i, h)),
        out_shape=jax.ShapeDtypeStruct((batch, seq, N_HEADS * V_DIM), BF16),
        scratch_shapes=[
            pltpu.VMEM((2, ATTN_TILE, tk), F32), pltpu.VMEM((2, ATTN_TILE, LANES), F32),
            pltpu.VMEM((2, ATTN_TILE, LANES), F32), pltpu.VMEM((2, ATTN_TILE, LANES), F32),
            pltpu.VMEM((2, ATTN_TILE, V_DIM), F32),
        ],
        compiler_params=_params("parallel", "parallel", "arbitrary"),
        name="diff_attention",
    )(jnp.full((1,), lam_init, F32), lam_l, subln_g_l.reshape(1, V_DIM), qkv3, qkv3, qkv3, band)
    return out.reshape(batch * seq, N_HEADS * V_DIM)


FNET_S2 = 128
ROW_TILE = 8


def _angles(j, k, n):
    return ((j * k) % n).astype(F32) * (2.0 * math.pi / n)


def _chan_dft_table():
    n = FGROUP_DIM
    ang = _angles(lax.broadcasted_iota(jnp.int32, (n, n), 0), lax.broadcasted_iota(jnp.int32, (n, n), 1), n)
    return (jnp.concatenate([jnp.cos(ang), jnp.sin(ang)], axis=1) * n ** -0.5).astype(BF16)


def _seq_dft_tables(seq):
    s2 = min(FNET_S2, seq)
    s1 = seq // s2
    r = min(ROW_TILE, s1)
    eye1 = jnp.eye(ROW_TILE, dtype=F32)
    eye2 = jnp.eye(r, dtype=F32)
    a1 =_angles(lax.broadcasted_iota(jnp.int32, (s1, s1), 0), lax.broadcasted_iota(jnp.int32, (s1, s1), 1), s1)
    c1, sn1 = jnp.cos(a1), jnp.sin(a1)
    m1 = jnp.concatenate([jnp.concatenate([c1, -sn1], axis=1), jnp.concatenate([-sn1, -c1], axis=1)], axis=0)
    m1 = m1 * s1 ** -0.5
    k1m = (m1[:, None, :, None] * eye1[None, :, None, :]).reshape(2 * s1 * ROW_TILE, 2 * s1 * ROW_TILE)

    k1 = lax.broadcasted_iota(jnp.int32, (s1, s2, s2), 0)
    k2 = lax.broadcasted_iota(jnp.int32, (s1, s2, s2), 1)
    n2 = lax.broadcasted_iota(jnp.int32, (s1, s2, s2), 2)
    a2 = _angles(k1 + s1 * k2, n2, seq)
    g = jnp.stack([jnp.cos(a2), jnp.sin(a2)], axis=0) * s2 ** -0.5
    g = g.reshape(2, s1 // r, r, s2, s2)
    k2m = jnp.einsum("pgjkn,jq->gkjpqn", g, eye2).reshape(s1 // r, s2 * r, 2 * r * s2)
    return k1m.astype(BF16), k2m.astype(BF16)


def _chan_dft_kernel(f_ref, cs_ref, o_ref):
    cs = cs_ref[...]
    for g in range(N_FGROUPS):
        lo = g * FGROUP_DIM
        z = jnp.dot(f_ref[:, lo:lo + FGROUP_DIM], cs, preferred_element_type=F32)
        o_ref[0, :, lo:lo + FGROUP_DIM] = z[:, :FGROUP_DIM]
        o_ref[1, :, lo:lo + FGROUP_DIM] = z[:, FGROUP_DIM:]


def _seq_stage_kernel(m_ref, x_ref, o_ref):
    width = x_ref.shape[-1]
    x = x_ref[...].reshape(-1, width).astype(BF16)
    o_ref[...] = jnp.dot(m_ref[...], x, preferred_element_type=F32).reshape(o_ref.shape)


def _fourier_mix(qkvf, f_col_block, chan_cs, seq_tabs, batch, seq):
    fw = N_FGROUPS * FGROUP_DIM
    k1m, k2m = seq_tabs
    s2 = min(FNET_S2, seq)
    s1 = seq // s2
    r = min(ROW_TILE, s1)
    assert s2 % ROW_TILE == 0 and s1 % r == 0
    bm = _tile(seq, 1024)
    qkv3 = qkvf.reshape(batch, seq, qkvf.shape[-1])
    z = pl.pallas_call(
        _chan_dft_kernel,
        grid=(batch, seq // bm),
        in_specs=[
            pl.BlockSpec((None, bm, fw), lambda b, i: (b, i, f_col_block)),
            pl.BlockSpec((FGROUP_DIM, 2 * FGROUP_DIM), lambda b, i: (0, 0)),
        ],
        out_specs=pl.BlockSpec((None, 2, bm, fw), lambda b, i: (b, 0, i, 0)),
        out_shape=jax.ShapeDtypeStruct((batch, 2, seq, fw), F32),
        compiler_params=_params("parallel", "parallel"),
        name="fnet_channel_dft",
    )(qkv3, chan_cs)

    a = pl.pallas_call(
        _seq_stage_kernel,
        grid=(batch, s2 // ROW_TILE),
        in_specs=[
            pl.BlockSpec(k1m.shape, lambda b, t: (0, 0)),
            pl.BlockSpec((None, 2, s1, ROW_TILE, fw), lambda b, t: (b, 0, 0, t, 0)),
        ],
        out_specs=pl.BlockSpec((None, 2, s1, ROW_TILE, fw), lambda b, t: (b, 0, 0, t, 0)),
        out_shape=jax.ShapeDtypeStruct((batch, 2, s1, s2, fw), F32),
        compiler_params=_params("parallel", "parallel"),
        name="fnet_seq_dft_stage1",
    )(k1m, z.reshape(batch, 2, s1, s2, fw))

    y = pl.pallas_call(
        _seq_stage_kernel,
        grid=(batch, s1 // r),
        in_specs=[
            pl.BlockSpec((None,) + k2m.shape[1:], lambda b, g: (g, 0, 0)),
            pl.BlockSpec((None, 2, r, s2, fw), lambda b, g: (b, 0, g, 0, 0)),
        ],
        out_specs=pl.BlockSpec((None, s2, r, fw), lambda b, g: (b, 0, g, 0)),
        out_shape=jax.ShapeDtypeStruct((batch, s2, s1, fw), F32),
        compiler_params=_params("parallel", "parallel"),
        name="fnet_seq_dft_stage2",
    )(k2m, a)
    return y.reshape(batch * seq, fw)


def _merge_kernel(o_ref, f_ref, wa_ref, wf_ref, g1_ref, g2_ref, out_ref):
    ya = jnp.dot(o_ref[...], wa_ref[...], preferred_element_type=F32)
    yf = jnp.dot(f_ref[...].astype(BF16), wf_ref[...], preferred_element_type=F32)
    out_ref[...] = (g1_ref[...].astype(F32) * ya + g2_ref[...].astype(F32) * yf).astype(out_ref.dtype)


def _gated_merge(o, fm, gates, w_attn, w_fnet, layer):
    t, ka = o.shape
    kf = fm.shape[1]
    d = w_attn.shape[-1]
    bm = _tile(t, 1024)
    bn = _tile(d, 512)
    nb = d // bn
    return pl.pallas_call(
        _merge_kernel,
        grid=(nb, t // bm),
        in_specs=[
            pl.BlockSpec((bm, ka), lambda j, i: (i, 0)),
            pl.BlockSpec((bm, kf), lambda j, i: (i, 0)),
            pl.BlockSpec((None, ka, bn), lambda j, i: (layer, 0, j)),
            pl.BlockSpec((None, kf, bn), lambda j, i: (layer, 0, j)),
            pl.BlockSpec((bm, bn), lambda j, i: (i, j)),
            pl.BlockSpec((bm, bn), lambda j, i: (i, nb + j)),
        ],
        out_specs=pl.BlockSpec((bm, bn), lambda j, i: (i, j)),
        out_shape=jax.ShapeDtypeStruct((t, d), BF16),
        compiler_params=_params("parallel", "parallel"),
        name="gated_merge",
    )(o, fm, w_attn, w_fnet, gates, gates)


def _proj_res_ln_kernel(a_ref, w_ref, x_ref, g_ref, b_ref, of_ref, ob_ref, acc_ref, *, alpha):
    k = pl.program_id(1)

    @pl.when(k == 0)
    def _():
        acc_ref[...] = jnp.zeros_like(acc_ref)

    acc_ref[...] += jnp.dot(a_ref[...], w_ref[...], preferred_element_type=F32)

    @pl.when(k == pl.num_programs(1) - 1)
    def _():
        y = _layernorm_rows(alpha * x_ref[...] + acc_ref[...], g_ref[...], b_ref[...])
        of_ref[...] = y
        ob_ref[...] = y.astype(BF16)


def _proj_residual_layernorm(a, w, layer, xf, g, b, alpha, bk_pref):
    t, kdim = a.shape
    d = w.shape[-1]
    bm = _tile(t, 512)
    bk = _tile(kdim, bk_pref)
    row = pl.BlockSpec((bm, d), lambda i, k: (i, 0))
    vec = pl.BlockSpec((None, 1, d), lambda i, k: (layer, 0, 0))
    return pl.pallas_call(
        functools.partial(_proj_res_ln_kernel, alpha=alpha),
        grid=(t // bm, kdim // bk),
        in_specs=[
            pl.BlockSpec((bm, bk), lambda i, k: (i, k)),
            pl.BlockSpec((None, bk, d), lambda i, k: (layer, k, 0)),
            row, vec, vec,
        ],
        out_specs=[row, row],
        out_shape=[jax.ShapeDtypeStruct((t, d), F32), jax.ShapeDtypeStruct((t, d), BF16)],
        scratch_shapes=[pltpu.VMEM((bm, d), F32)],
        compiler_params=_params("parallel", "arbitrary"),
        name="proj_residual_layernorm",
    )(a, w, xf, g, b)


def _swiglu_kernel(x_ref, wg_ref, wu_ref, o_ref):
    x = x_ref[...]
    gate = jnp.dot(x, wg_ref[...], preferred_element_type=F32)
    up = jnp.dot(x, wu_ref[...], preferred_element_type=F32)
    o_ref[...] = (gate * jax.nn.sigmoid(gate) * up).astype(o_ref.dtype)


def _swiglu_up(xb, w_gu, layer, d_ff):
    t, d = xb.shape
    bm = _tile(t, 1024)
    bn = _tile(d_ff, 512)
    nb = d_ff // bn
    return pl.pallas_call(
        _swiglu_kernel,
        grid=(nb, t // bm),
        in_specs=[
            pl.BlockSpec((bm, d), lambda j, i: (i, 0)),
            pl.BlockSpec((None, d, bn), lambda j, i: (layer, 0, j)),
            pl.BlockSpec((None, d, bn), lambda j, i: (layer, 0, nb + j)),
        ],
        out_specs=pl.BlockSpec((bm, bn), lambda j, i: (i, j)),
        out_shape=jax.ShapeDtypeStruct((t, d_ff), BF16),
        compiler_params=_params("parallel", "parallel"),
        name="swiglu_up",
    )(xb, w_gu, w_gu)


def _trunk(x, band, chan_cs, seq_tabs, p):
    batch, seq, d = x.shape
    depth = p["w_in"].shape[0]
    alpha = (2.0 * depth) ** 0.25
    qk_width = N_HEADS * 2 * HEAD_DIM
    attn_width = N_HEADS * V_DIM
    fnet_width = N_FGROUPS * FGROUP_DIM
    n_main = 2 * qk_width + attn_width + fnet_width
    assert (2 * qk_width + attn_width) % fnet_width == 0
    f_col_block = (2 * qk_width + attn_width) // fnet_width
    d_ff = p["w_down"].shape[1]

    col_scale = jnp.concatenate(
        [jnp.full((qk_width,), HEAD_DIM ** -0.5 * LOG2_E, F32), jnp.ones((n_main - qk_width,), F32)]).reshape(1, n_main)

    xf, xb = _input_layernorm(x.reshape(batch * seq, d), p["ln_in_g"], p["ln_in_b"])
    for l in range(depth):
        lam_init = 0.8 - 0.6 * math.exp(-0.3 * l)
        qkvf, gates = _in_projection(xb, p["w_in"], l, col_scale, p["b_gate"], n_main)
        o = _diff_attention(qkvf, p["lam"][l], p["subln_g"][l], band, batch, seq, lam_init)
        fm = _fourier_mix(qkvf, f_col_block, chan_cs, seq_tabs, batch, seq)
        merged = _gated_merge(o, fm, gates, p["w_br_attn"], p["w_br_fnet"], l)
        xf, xb = _proj_residual_layernorm(merged, p["w_out"], l, xf, p["ln1_g"], p["ln1_b"], alpha, 2048)
        act = _swiglu_up(xb, p["w_gu"], l, d_ff)
        xf, xb = _proj_residual_layernorm(act, p["w_down"], l, xf, p["ln2_g"], p["ln2_b"], alpha, 2816)
    return xf.reshape(batch, seq, d)


def kernel(x_prompt, x_sample, rel_bias, ln_in_g, ln_in_b, w_in, b_gate, lam, subln_g, w_br_attn, w_br_fnet,
           w_out, ln1_g, ln1_b, w_gu, w_down, ln2_g, ln2_b):
    depth, d = ln1_g.shape
    p = {
        "ln_in_g": ln_in_g, "ln_in_b": ln_in_b,
        "w_in": w_in.astype(BF16), "b_gate": b_gate.reshape(depth, 1, -1),
        "lam": lam, "subln_g": subln_g,
        "w_br_attn": w_br_attn.astype(BF16), "w_br_fnet": w_br_fnet.astype(BF16),
        "w_out": w_out.astype(BF16), "w_gu": w_gu.astype(BF16), "w_down": w_down.astype(BF16),
        "ln1_g": ln1_g.reshape(depth, 1, d), "ln1_b": ln1_b.reshape(depth, 1, d),
        "ln2_g": ln2_g.reshape(depth, 1, d), "ln2_b": ln2_b.reshape(depth, 1, d),
    }
    band = _bias_tiles(rel_bias)
    chan_cs = _chan_dft_table()
    return tuple(_trunk(x, band, chan_cs, _seq_dft_tables(x.shape[1]), p) for x in (x_prompt, x_sample))
```

```python
import functools
import math

import jax
import jax.numpy as jnp
from jax import lax
from jax.experimental import pallas as pl
from jax.experimental.pallas import tpu as pltpu

N_HEADS = 8
HEAD_DIM = 128
V_DIM = 2 * HEAD_DIM
N_FGROUPS = 8
FGROUP_DIM = 128
N_BUCKETS = 32
MAX_DISTANCE = 128
LN_EPS = 1e-5
LOG2_E = math.log2(math.e)

LANES = 128
ATTN_TILE = 256
ATTN_ROWS_PER_STEP = 2048
VMEM_LIMIT_BYTES = 56 * 1024 * 1024

F32 = jnp.float32
BF16 = jnp.bfloat16


def _tile(n, pref):
    if n <= pref:
        return n
    t = (pref // LANES) * LANES
    while t >= LANES:
        if n % t == 0:
            return t
        t -= LANES
    raise ValueError(f"no lane-aligned tile for {n}")


def _params(*sem):
    return pltpu.CompilerParams(dimension_semantics=sem, vmem_limit_bytes=VMEM_LIMIT_BYTES)


def _layernorm_rows(y, g, b):
    mu = jnp.mean(y, axis=-1, keepdims=True)
    yc = y - mu
    var = jnp.mean(yc * yc, axis=-1, keepdims=True)
    return yc * lax.rsqrt(var + LN_EPS) * g + b


def _ln_kernel(x_ref, g_ref, b_ref, of_ref, ob_ref):
    y = _layernorm_rows(x_ref[...], g_ref[...], b_ref[...])
    of_ref[...] = y
    ob_ref[...] = y.astype(BF16)


def _input_layernorm(x, g, b):
    t, d = x.shape
    bm = _tile(t, 512)
    row = pl.BlockSpec((bm, d), lambda i: (i, 0))
    vec = pl.BlockSpec((1, d), lambda i: (0, 0))
    return pl.pallas_call(
        _ln_kernel,
        grid=(t // bm,),
        in_specs=[row, vec, vec],
        out_specs=[row, row],
        out_shape=[jax.ShapeDtypeStruct((t, d), F32), jax.ShapeDtypeStruct((t, d), BF16)],
        compiler_params=_params("parallel"),
        name="input_layernorm",
    )(x, g.reshape(1, d), b.reshape(1, d))


def _proj_kernel(x_ref, w_ref, s_ref, o_ref):
    acc = jnp.dot(x_ref[...], w_ref[...], preferred_element_type=F32)
    o_ref[...] = (acc * s_ref[...]).astype(o_ref.dtype)


def _gate_kernel(x_ref, w_ref, b_ref, o_ref):
    acc = jnp.dot(x_ref[...], w_ref[...], preferred_element_type=F32)
    o_ref[...] = jax.nn.sigmoid(acc + b_ref[...]).astype(o_ref.dtype)


def _in_projection(xb, w_in, layer, col_scale, b_gate, n_main):
    t, d = xb.shape
    n_gate = w_in.shape[-1] - n_main
    bm = _tile(t, 1024)

    bn = _tile(n_main, 1024)
    qkvf = pl.pallas_call(
        _proj_kernel,
        grid=(n_main // bn, t // bm),
        in_specs=[
            pl.BlockSpec((bm, d), lambda j, i: (i, 0)),
            pl.BlockSpec((None, d, bn), lambda j, i: (layer, 0, j)),
            pl.BlockSpec((1, bn), lambda j, i: (0, j)),
        ],
        out_specs=pl.BlockSpec((bm, bn), lambda j, i: (i, j)),
        out_shape=jax.ShapeDtypeStruct((t, n_main), BF16),
        compiler_params=_params("parallel", "parallel"),
        name="in_proj_qkvf",
    )(xb, w_in, col_scale)

    bg = _tile(n_gate, 1024)
    assert n_main % bg == 0
    off = n_main // bg
    gates = pl.pallas_call(
        _gate_kernel,
        grid=(n_gate // bg, t // bm),
        in_specs=[
            pl.BlockSpec((bm, d), lambda j, i: (i, 0)),
            pl.BlockSpec((None, d, bg), lambda j, i: (layer, 0, off + j)),
            pl.BlockSpec((None, 1, bg), lambda j, i: (layer, 0, j)),
        ],
        out_specs=pl.BlockSpec((bm, bg), lambda j, i: (i, j)),
        out_shape=jax.ShapeDtypeStruct((t, n_gate), BF16),
        compiler_params=_params("parallel", "parallel"),
        name="in_proj_gates",
    )(xb, w_in, b_gate)
    return qkvf, gates


def _t5_bucket(rel):
    nb = N_BUCKETS // 2
    ret = jnp.where(rel > 0, nb, 0)
    n = jnp.abs(rel)
    max_exact = nb // 2
    nf = jnp.maximum(n, 1).astype(F32)
    large = max_exact + (jnp.log(nf / max_exact) / math.log(MAX_DISTANCE / max_exact) * (nb - max_exact)).astype(jnp.int32)
    large = jnp.minimum(large, nb - 1)
    return ret + jnp.where(n < max_exact, n, large)


N_BIAS_TILES = 5


def _bias_tile_kernel(tab_ref, ids_ref, o_ref):
    h = pl.program_id(0)
    ids = ids_ref[...]
    acc = jnp.zeros(ids.shape, F32)
    for b in range(N_BUCKETS):
        acc = jnp.where(ids == b, tab_ref[b, h], acc)
    o_ref[...] = acc * LOG2_E


def _bias_tiles(rel_bias):
    tb = ATTN_TILE
    assert tb >= MAX_DISTANCE
    i = lax.broadcasted_iota(jnp.int32, (N_BIAS_TILES, tb, tb), 1)
    j = lax.broadcasted_iota(jnp.int32, (N_BIAS_TILES, tb, tb), 2)
    d = lax.broadcasted_iota(jnp.int32, (N_BIAS_TILES, tb, tb), 0) - 2
    ids = _t5_bucket(d * tb + j - i).reshape(N_BIAS_TILES * tb, tb)
    out = pl.pallas_call(
        _bias_tile_kernel,
        grid=(N_HEADS,),
        in_specs=[
            pl.BlockSpec(memory_space=pltpu.SMEM),
            pl.BlockSpec((N_BIAS_TILES * tb, tb), lambda h: (0, 0)),
        ],
        out_specs=pl.BlockSpec((None, N_BIAS_TILES * tb, tb), lambda h: (h, 0, 0)),
        out_shape=jax.ShapeDtypeStruct((N_HEADS, N_BIAS_TILES * tb, tb), F32),
        compiler_params=_params("parallel"),
        name="t5_bias_tiles",
    )(rel_bias, ids)
    return out.reshape(N_HEADS, N_BIAS_TILES, tb, tb)


def _lane_wide(x, width):
    return jnp.concatenate([x] * (width // LANES), axis=1)


def _attn_kernel(lam_init_ref, lam_ref, g_ref, q_ref, k_ref, v_ref, band_ref, o_ref, s_ref, mx_ref,
                 *, n_items, n_keys):
    tq = ATTN_TILE
    sub = n_keys // ATTN_TILE
    maps = 2
    assert n_items >= 2
    group = pl.program_id(2)
    nt = (((1,), (1,)), ((), ()))

    lam_init = lam_init_ref[0]
    lam = lam_ref[...]
    lam_full = (jnp.exp(jnp.sum(lam[0:1] * lam[1:2], axis=-1, keepdims=True))
                - jnp.exp(jnp.sum(lam[2:3] * lam[3:4], axis=-1, keepdims=True)) + lam_init)
    gain = g_ref[...] * (1.0 - lam_init)

    def rows_of(i):
        return pl.ds(i * tq, tq) if isinstance(i, int) else pl.ds(pl.multiple_of(i * tq, tq), tq)

    def qk_cols(m):
        return slice(m * HEAD_DIM, (m + 1) * HEAD_DIM)

    def half_max(x):
        return jnp.maximum(x[:, :LANES], x[:, LANES:])

    def half_sum(x):
        return x[:, :LANES] + x[:, LANES:]

    def run(a_item, b_item):
        if a_item is not None:
            a_row = group * n_items + a_item
            a_q = [q_ref[rows_of(a_item), qk_cols(m)] for m in range(maps)]
            a_max = [None] * maps
        if b_item is not None:
            b_m_wide = [_lane_wide(mx_ref[m], ATTN_TILE) for m in range(maps)]
            b_sum = [None] * maps
            b_acc = [None] * maps

        for u in range(sub):
            cols = slice(u * ATTN_TILE, (u + 1) * ATTN_TILE)
            if b_item is not None:
                vu = v_ref[cols, :]
                for m in range(maps):
                    p = jnp.exp2(s_ref[m, :, cols] - b_m_wide[m])
                    b_sum[m] = half_sum(p) if b_sum[m] is None else b_sum[m] + half_sum(p)
                    pv = jnp.dot(p.astype(BF16), vu, preferred_element_type=F32)
                    b_acc[m] = pv if b_acc[m] is None else pv + b_acc[m]
            if a_item is not None:
                tile = band_ref[jnp.clip(u - a_row, -2, 2) + 2]
                for m in range(maps):
                    s = lax.dot_general(a_q[m], k_ref[cols, qk_cols(m)], nt, preferred_element_type=F32) + tile
                    s_ref[m, :, cols] = s
                    a_max[m] = half_max(s) if a_max[m] is None else jnp.maximum(a_max[m], half_max(s))

        if a_item is not None:
            for m in range(maps):
                mx_ref[m] = jnp.broadcast_to(jnp.max(a_max[m], axis=-1, keepdims=True), (tq, LANES))
        if b_item is not None:
            o1 = b_acc[0] / jnp.sum(b_sum[0], axis=-1, keepdims=True)
            o2 = b_acc[1] / jnp.sum(b_sum[1], axis=-1, keepdims=True)
            of = o1 - lam_full * o2
            of = of * lax.rsqrt(jnp.mean(of * of, axis=-1, keepdims=True) + LN_EPS)
            o_ref[rows_of(b_item), :] = (of * gain).astype(o_ref.dtype)

    run(0, None)

    def body(t, carry):
        run(t + 1, t)
        return carry

    lax.fori_loop(0, n_items - 1, body, 0)
    run(None, n_items - 1)


def _diff_attention(qkvf, lam_l, subln_g_l, band, batch, seq, lam_init):
    tq = _tile(seq, ATTN_ROWS_PER_STEP)
    assert tq % ATTN_TILE == 0 and seq % tq == 0 and seq % ATTN_TILE == 0
    qkv3 = qkvf.reshape(batch, seq, qkvf.shape[-1])
    kern = functools.partial(_attn_kernel, n_items=tq // ATTN_TILE, n_keys=seq)
    out = pl.pallas_call(
        kern,
        grid=(batch, N_HEADS, seq // tq),
        in_specs=[
            pl.BlockSpec(memory_space=pltpu.SMEM),
            pl.BlockSpec((4, HEAD_DIM), lambda b, h, i: (0, 0)),
            pl.BlockSpec((1, V_DIM), lambda b, h, i: (0, 0)),
            pl.BlockSpec((None, tq, V_DIM), lambda b, h, i: (b, i, h)),
            pl.BlockSpec((None, seq, V_DIM), lambda b, h, i: (b, 0, N_HEADS + h)),
            pl.BlockSpec((None, seq, V_DIM), lambda b, h, i: (b, 0, 2 * N_HEADS + h)),
            pl.BlockSpec((None, N_BIAS_TILES, ATTN_TILE, ATTN_TILE), lambda b, h, i: (h, 0, 0, 0)),
        ],
        out_specs=pl.BlockSpec((None, tq, V_DIM), lambda b, h, i: (b, i, h)),
        out_shape=jax.ShapeDtypeStruct((batch, seq, N_HEADS * V_DIM), BF16),
        scratch_shapes=[
            pltpu.VMEM((2, ATTN_TILE, seq), F32),
            pltpu.VMEM((2, ATTN_TILE, LANES), F32),
        ],
        compiler_params=_params("parallel", "parallel", "arbitrary"),
        name="diff_attention",
    )(jnp.full((1,), lam_init, F32), lam_l, subln_g_l.reshape(1, V_DIM), qkv3, qkv3, qkv3, band)
    return out.reshape(batch * seq, N_HEADS * V_DIM)


FNET_S2 = 128
ROW_TILE = 8
FNET_STAGE1_OUT_BYTES = 8 * 1024 * 1024


def _angles(j, k, n):
    return ((j * k) % n).astype(F32) * (2.0 * math.pi / n)


def _chan_dft_table():
    n = FGROUP_DIM
    ang = _angles(lax.broadcasted_iota(jnp.int32, (n, n), 0), lax.broadcasted_iota(jnp.int32, (n, n), 1), n)
    return (jnp.concatenate([jnp.cos(ang), jnp.sin(ang)], axis=1) * n ** -0.5).astype(BF16)


def _seq_dft_tables(seq):
    s2 = min(FNET_S2, seq)
    s1 = seq // s2
    r = min(ROW_TILE, s1)
    eye1 = jnp.eye(ROW_TILE, dtype=F32)
    eye2 = jnp.eye(r, dtype=F32)
    a1 = _angles(lax.broadcasted_iota(jnp.int32, (s1, s1), 0), lax.broadcasted_iota(jnp.int32, (s1, s1), 1), s1)
    c1, sn1 = jnp.cos(a1), jnp.sin(a1)
    m1 = jnp.concatenate([jnp.concatenate([c1, -sn1], axis=1), jnp.concatenate([-sn1, -c1], axis=1)], axis=0)
    m1 = m1 * s1 ** -0.5
    k1m = (m1[:, None, :, None] * eye1[None, :, None, :]).reshape(2 * s1 * ROW_TILE, 2 * s1 * ROW_TILE)

    k1 = lax.broadcasted_iota(jnp.int32, (s1, s2, s2), 0)
    k2 = lax.broadcasted_iota(jnp.int32, (s1, s2, s2), 1)
    n2 = lax.broadcasted_iota(jnp.int32, (s1, s2, s2), 2)
    a2 = _angles(k1 + s1 * k2, n2, seq)
    g = jnp.stack([jnp.cos(a2), jnp.sin(a2)], axis=0) * s2 ** -0.5
    g = g.reshape(2, s1 // r, r, s2, s2)
    k2m = jnp.einsum("pgjkn,jq->gkjpqn", g, eye2).reshape(s1 // r, s2 * r, 2 * r * s2)
    return k1m.astype(BF16), k2m.astype(BF16)


def _chan_stage1_kernel(cs_ref, m_ref, f_ref, o_ref, *, tiles):
    s1, rows, fw = f_ref.shape
    f = f_ref[...].reshape(s1 * rows, fw)
    cs = cs_ref[...]
    zc, zs = [], []
    for g in range(N_FGROUPS):
        z = jnp.dot(f[:, g * FGROUP_DIM:(g + 1) * FGROUP_DIM], cs, preferred_element_type=F32)
        zc.append(z[:, :FGROUP_DIM])
        zs.append(z[:, FGROUP_DIM:])
    z = jnp.stack([jnp.concatenate(zc, axis=1), jnp.concatenate(zs, axis=1)], axis=0)
    z = z.reshape(2, s1, tiles, ROW_TILE, fw)
    for t in range(tiles):
        x = z[:, :, t].reshape(2 * s1 * ROW_TILE, fw).astype(BF16)
        y = jnp.dot(m_ref[...], x, preferred_element_type=F32)
        o_ref[:, :, t * ROW_TILE:(t + 1) * ROW_TILE, :] = y.reshape(2, s1, ROW_TILE, fw)


def _seq_stage2_kernel(m_ref, x_ref, o_ref):
    width = x_ref.shape[-1]
    x = x_ref[...].reshape(-1, width).astype(BF16)
    o_ref[...] = jnp.dot(m_ref[...], x, preferred_element_type=F32).reshape(o_ref.shape)


def _fourier_mix(qkvf, f_col_block, chan_cs, seq_tabs, batch, seq):
    fw = N_FGROUPS * FGROUP_DIM
    k1m, k2m = seq_tabs
    s2 = min(FNET_S2, seq)
    s1 = seq // s2
    r = min(ROW_TILE, s1)
    assert s2 % ROW_TILE == 0 and s1 % r == 0
    n2_rows = max(2 * ROW_TILE, min(s2, FNET_STAGE1_OUT_BYTES // (2 * s1 * fw * 4)))
    assert s2 % n2_rows == 0 and n2_rows % (2 * ROW_TILE) == 0
    tiles = n2_rows // ROW_TILE
    qkv4 = qkvf.reshape(batch, s1, s2, qkvf.shape[-1])
    a = pl.pallas_call(
        functools.partial(_chan_stage1_kernel, tiles=tiles),
        grid=(batch, s2 // n2_rows),
        in_specs=[
            pl.BlockSpec((FGROUP_DIM, 2 * FGROUP_DIM), lambda b, t: (0, 0)),
            pl.BlockSpec(k1m.shape, lambda b, t: (0, 0)),
            pl.BlockSpec((None, s1, n2_rows, fw), lambda b, t: (b, 0, t, f_col_block)),
        ],
        out_specs=pl.BlockSpec((None, 2, s1, n2_rows, fw), lambda b, t: (b, 0, 0, t, 0)),
        out_shape=jax.ShapeDtypeStruct((batch, 2, s1, s2, fw), F32),
        compiler_params=_params("parallel", "parallel"),
        name="fnet_chan_dft_stage1",
    )(chan_cs, k1m, qkv4)

    y = pl.pallas_call(
        _seq_stage2_kernel,
        grid=(batch, s1 // r),
        in_specs=[
            pl.BlockSpec((None,) + k2m.shape[1:], lambda b, g: (g, 0, 0)),
            pl.BlockSpec((None, 2, r, s2, fw), lambda b, g: (b, 0, g, 0, 0)),
        ],
        out_specs=pl.BlockSpec((None, s2, r, fw), lambda b, g: (b, 0, g, 0)),
        out_shape=jax.ShapeDtypeStruct((batch, s2, s1, fw), F32),
        compiler_params=_params("parallel", "parallel"),
        name="fnet_seq_dft_stage2",
    )(k2m, a)
    return y.reshape(batch * seq, fw)


def _merge_kernel(o_ref, f_ref, wa_ref, wf_ref, g1_ref, g2_ref, out_ref):
    ya = jnp.dot(o_ref[...], wa_ref[...], preferred_element_type=F32)
    yf = jnp.dot(f_ref[...].astype(BF16), wf_ref[...], preferred_element_type=F32)
    out_ref[...] = (g1_ref[...].astype(F32) * ya + g2_ref[...].astype(F32) * yf).astype(out_ref.dtype)


def _gated_merge(o, fm, gates, w_attn, w_fnet, layer):
    t, ka = o.shape
    kf = fm.shape[1]
    d = w_attn.shape[-1]
    bm = _tile(t, 1024)
    bn = _tile(d, 512)
    nb = d // bn
    return pl.pallas_call(
        _merge_kernel,
        grid=(nb, t // bm),
        in_specs=[
            pl.BlockSpec((bm, ka), lambda j, i: (i, 0)),
            pl.BlockSpec((bm, kf), lambda j, i: (i, 0)),
            pl.BlockSpec((None, ka, bn), lambda j, i: (layer, 0, j)),
            pl.BlockSpec((None, kf, bn), lambda j, i: (layer, 0, j)),
            pl.BlockSpec((bm, bn), lambda j, i: (i, j)),
            pl.BlockSpec((bm, bn), lambda j, i: (i, nb + j)),
        ],
        out_specs=pl.BlockSpec((bm, bn), lambda j, i: (i, j)),
        out_shape=jax.ShapeDtypeStruct((t, d), BF16),
        compiler_params=_params("parallel", "parallel"),
        name="gated_merge",
    )(o, fm, w_attn, w_fnet, gates, gates)


def _proj_res_ln_kernel(a_ref, w_ref, x_ref, g_ref, b_ref, of_ref, ob_ref, acc_ref, *, alpha):
    k = pl.program_id(1)

    @pl.when(k == 0)
    def _():
        acc_ref[...] = jnp.zeros_like(acc_ref)

    acc_ref[...] += jnp.dot(a_ref[...], w_ref[...], preferred_element_type=F32)

    @pl.when(k == pl.num_programs(1) - 1)
    def _():
        y = _layernorm_rows(alpha * x_ref[...] + acc_ref[...], g_ref[...], b_ref[...])
        of_ref[...] = y
        ob_ref[...] = y.astype(BF16)


def _proj_residual_layernorm(a, w, layer, xf, g, b, alpha, bk_pref):
    t, kdim = a.shape
    d = w.shape[-1]
    bm = _tile(t, 512)
    bk = _tile(kdim, bk_pref)
    row = pl.BlockSpec((bm, d), lambda i, k: (i, 0))
    vec = pl.BlockSpec((None, 1, d), lambda i, k: (layer, 0, 0))
    return pl.pallas_call(
        functools.partial(_proj_res_ln_kernel, alpha=alpha),
        grid=(t // bm, kdim // bk),
        in_specs=[
            pl.BlockSpec((bm, bk), lambda i, k: (i, k)),
            pl.BlockSpec((None, bk, d), lambda i, k: (layer, k, 0)),
            row, vec, vec,
        ],
        out_specs=[row, row],
        out_shape=[jax.ShapeDtypeStruct((t, d), F32), jax.ShapeDtypeStruct((t, d), BF16)],
        scratch_shapes=[pltpu.VMEM((bm, d), F32)],
        compiler_params=_params("parallel", "arbitrary"),
        name="proj_residual_layernorm",
    )(a, w, xf, g, b)


def _swiglu_kernel(x_ref, wg_ref, wu_ref, o_ref):
    x = x_ref[...]
    gate = jnp.dot(x, wg_ref[...], preferred_element_type=F32)
    up = jnp.dot(x, wu_ref[...], preferred_element_type=F32)
    o_ref[...] = (gate * jax.nn.sigmoid(gate) * up).astype(o_ref.dtype)


def _swiglu_up(xb, w_gu, layer, d_ff):
    t, d = xb.shape
    bm = _tile(t, 1024)
    bn = _tile(d_ff, 512)
    nb = d_ff // bn
    return pl.pallas_call(
        _swiglu_kernel,
        grid=(nb, t // bm),
        in_specs=[
            pl.BlockSpec((bm, d), lambda j, i: (i, 0)),
            pl.BlockSpec((None, d, bn), lambda j, i: (layer, 0, j)),
            pl.BlockSpec((None, d, bn), lambda j, i: (layer, 0, nb + j)),
        ],
        out_specs=pl.BlockSpec((bm, bn), lambda j, i: (i, j)),
        out_shape=jax.ShapeDtypeStruct((t, d_ff), BF16),
        compiler_params=_params("parallel", "parallel"),
        name="swiglu_up",
    )(xb, w_gu, w_gu)


def _trunk(x, band, chan_cs, seq_tabs, p):
    batch, seq, d = x.shape
    depth = p["w_in"].shape[0]
    alpha = (2.0 * depth) ** 0.25
    qk_width = N_HEADS * 2 * HEAD_DIM
    attn_width = N_HEADS * V_DIM
    fnet_width = N_FGROUPS * FGROUP_DIM
    n_main = 2 * qk_width + attn_width + fnet_width
    assert (2 * qk_width + attn_width) % fnet_width == 0
    f_col_block = (2 * qk_width + attn_width) // fnet_width
    d_ff = p["w_down"].shape[1]

    col_scale = jnp.concatenate(
        [jnp.full((qk_width,), HEAD_DIM ** -0.5 * LOG2_E, F32), jnp.ones((n_main - qk_width,), F32)]).reshape(1, n_main)

    xf, xb = _input_layernorm(x.reshape(batch * seq, d), p["ln_in_g"], p["ln_in_b"])
    for l in range(depth):
        lam_init = 0.8 - 0.6 * math.exp(-0.3 * l)
        qkvf, gates = _in_projection(xb, p["w_in"], l, col_scale, p["b_gate"], n_main)
        o = _diff_attention(qkvf, p["lam"][l], p["subln_g"][l], band, batch, seq, lam_init)
        fm = _fourier_mix(qkvf, f_col_block, chan_cs, seq_tabs, batch, seq)
        merged = _gated_merge(o, fm, gates, p["w_br_attn"], p["w_br_fnet"], l)
        xf, xb = _proj_residual_layernorm(merged, p["w_out"], l, xf, p["ln1_g"], p["ln1_b"], alpha, 2048)
        act = _swiglu_up(xb, p["w_gu"], l, d_ff)
        xf, xb = _proj_residual_layernorm(act, p["w_down"], l, xf, p["ln2_g"], p["ln2_b"], alpha, 2816)
    return xf.reshape(batch, seq, d)


def kernel(x_prompt, x_sample, rel_bias, ln_in_g, ln_in_b, w_in, b_gate, lam, subln_g, w_br_attn, w_br_fnet,
           w_out, ln1_g, ln1_b, w_gu, w_down, ln2_g, ln2_b):
    depth, d = ln1_g.shape
    p = {
        "ln_in_g": ln_in_g, "ln_in_b": ln_in_b,
        "w_in": w_in.astype(BF16), "b_gate": b_gate.reshape(depth, 1, -1),
        "lam": lam, "subln_g": subln_g,
        "w_br_attn": w_br_attn.astype(BF16), "w_br_fnet": w_br_fnet.astype(BF16),
        "w_out": w_out.astype(BF16), "w_gu": w_gu.astype(BF16), "w_down": w_down.astype(BF16),
        "ln1_g": ln1_g.reshape(depth, 1, d), "ln1_b": ln1_b.reshape(depth, 1, d),
        "ln2_g": ln2_g.reshape(depth, 1, d), "ln2_b": ln2_b.reshape(depth, 1, d),
    }
    band = _bias_tiles(rel_bias)
    chan_cs = _chan_dft_table()
    return tuple(_trunk(x, band, chan_cs, _seq_dft_tables(x.shape[1]), p) for x in (x_prompt, x_sample))
```

```python
import functools
import math

import jax
import jax.numpy as jnp
from jax import lax
from jax.experimental import pallas as pl
from jax.experimental.pallas import tpu as pltpu

N_HEADS = 8
HEAD_DIM = 128
V_DIM = 2 * HEAD_DIM
N_FGROUPS = 8
FGROUP_DIM = 128
N_BUCKETS = 32
MAX_DISTANCE = 128
LN_EPS = 1e-5
LOG2_E = math.log2(math.e)

LANES = 128
ATTN_TILE = 256
ATTN_ROWS_PER_STEP = 2048
VMEM_LIMIT_BYTES = 56 * 1024 * 1024

F32 = jnp.float32
BF16 = jnp.bfloat16


def _tile(n, pref):
    if n <= pref:
        return n
    t = (pref // LANES) * LANES
    while t >= LANES:
        if n % t == 0:
            return t
        t -= LANES
    raise ValueError(f"no lane-aligned tile for {n}")


def _params(*sem):
    return pltpu.CompilerParams(dimension_semantics=sem, vmem_limit_bytes=VMEM_LIMIT_BYTES)


def _layernorm_rows(y, g, b):
    mu = jnp.mean(y, axis=-1, keepdims=True)
    yc = y - mu
    var = jnp.mean(yc * yc, axis=-1, keepdims=True)
    return yc * lax.rsqrt(var + LN_EPS) * g + b


def _ln_kernel(x_ref, g_ref, b_ref, of_ref, ob_ref):
    y = _layernorm_rows(x_ref[...], g_ref[...], b_ref[...])
    of_ref[...] = y
    ob_ref[...] = y.astype(BF16)


def _input_layernorm(x, g, b):
    t, d = x.shape
    bm = _tile(t, 512)
    row = pl.BlockSpec((bm, d), lambda i: (i, 0))
    vec = pl.BlockSpec((1, d), lambda i: (0, 0))
    return pl.pallas_call(
        _ln_kernel,
        grid=(t // bm,),
        in_specs=[row, vec, vec],
        out_specs=[row, row],
        out_shape=[jax.ShapeDtypeStruct((t, d), F32), jax.ShapeDtypeStruct((t, d), BF16)],
        compiler_params=_params("parallel"),
        name="input_layernorm",
    )(x, g.reshape(1, d), b.reshape(1, d))


def _proj_kernel(x_ref, w_ref, s_ref, o_ref):
    acc = jnp.dot(x_ref[...], w_ref[...], preferred_element_type=F32)
    o_ref[...] = (acc * s_ref[...]).astype(o_ref.dtype)


def _gate_kernel(x_ref, w_ref, b_ref, o_ref):
    acc = jnp.dot(x_ref[...], w_ref[...], preferred_element_type=F32)
    o_ref[...] = jax.nn.sigmoid(acc + b_ref[...]).astype(o_ref.dtype)


def _in_projection(xb, w_in, layer, col_scale, b_gate, n_main):
    t, d = xb.shape
    n_gate = w_in.shape[-1] - n_main
    bm = _tile(t, 1024)

    bn = _tile(n_main, 1024)
    qkvf = pl.pallas_call(
        _proj_kernel,
        grid=(n_main // bn, t // bm),
        in_specs=[
            pl.BlockSpec((bm, d), lambda j, i: (i, 0)),
            pl.BlockSpec((None, d, bn), lambda j, i: (layer, 0, j)),
            pl.BlockSpec((1, bn), lambda j, i: (0, j)),
        ],
        out_specs=pl.BlockSpec((bm, bn), lambda j, i: (i, j)),
        out_shape=jax.ShapeDtypeStruct((t, n_main), BF16),
        compiler_params=_params("parallel", "parallel"),
        name="in_proj_qkvf",
    )(xb, w_in, col_scale)

    bg = _tile(n_gate, 1024)
    assert n_main % bg == 0
    off = n_main // bg
    gates = pl.pallas_call(
        _gate_kernel,
        grid=(n_gate // bg, t // bm),
        in_specs=[
            pl.BlockSpec((bm, d), lambda j, i: (i, 0)),
            pl.BlockSpec((None, d, bg), lambda j, i: (layer, 0, off + j)),
            pl.BlockSpec((None, 1, bg), lambda j, i: (layer, 0, j)),
        ],
        out_specs=pl.BlockSpec((bm, bg), lambda j, i: (i, j)),
        out_shape=jax.ShapeDtypeStruct((t, n_gate), BF16),
        compiler_params=_params("parallel", "parallel"),
        name="in_proj_gates",
    )(xb, w_in, b_gate)
    return qkvf, gates


def _t5_bucket(rel):
    nb = N_BUCKETS // 2
    ret = jnp.where(rel > 0, nb, 0)
    n = jnp.abs(rel)
    max_exact = nb // 2
    nf = jnp.maximum(n, 1).astype(F32)
    large = max_exact + (jnp.log(nf / max_exact) / math.log(MAX_DISTANCE / max_exact) * (nb - max_exact)).astype(jnp.int32)
    large = jnp.minimum(large, nb - 1)
    return ret + jnp.where(n < max_exact, n, large)


N_BIAS_TILES = 5


def _bias_tile_kernel(tab_ref, ids_ref, o_ref):
    h = pl.program_id(0)
    ids = ids_ref[...]
    acc = jnp.zeros(ids.shape, F32)
    for b in range(N_BUCKETS):
        acc = jnp.where(ids == b, tab_ref[b, h], acc)
    o_ref[...] = acc * LOG2_E


def _bias_tiles(rel_bias):
    tb = ATTN_TILE
    assert tb >= MAX_DISTANCE
    i = lax.broadcasted_iota(jnp.int32, (N_BIAS_TILES, tb, tb), 1)
    j = lax.broadcasted_iota(jnp.int32, (N_BIAS_TILES, tb, tb), 2)
    d = lax.broadcasted_iota(jnp.int32, (N_BIAS_TILES, tb, tb), 0) - 2
    ids = _t5_bucket(d * tb + j - i).reshape(N_BIAS_TILES * tb, tb)
    out = pl.pallas_call(
        _bias_tile_kernel,
        grid=(N_HEADS,),
        in_specs=[
            pl.BlockSpec(memory_space=pltpu.SMEM),
            pl.BlockSpec((N_BIAS_TILES * tb, tb), lambda h: (0, 0)),
        ],
        out_specs=pl.BlockSpec((None, N_BIAS_TILES * tb, tb), lambda h: (h, 0, 0)),
        out_shape=jax.ShapeDtypeStruct((N_HEADS, N_BIAS_TILES * tb, tb), F32),
        compiler_params=_params("parallel"),
        name="t5_bias_tiles",
    )(rel_bias, ids)
    return out.reshape(N_HEADS, N_BIAS_TILES, tb, tb)


def _lane_wide(x, width):
    return jnp.concatenate([x] * (width // LANES), axis=1)


def _attn_kernel(lam_init_ref, lam_ref, g_ref, q_ref, k_ref, v_ref, band_ref, o_ref, s_ref, mx_ref,
                 *, n_items, n_keys):
    tq = ATTN_TILE
    sub = n_keys // ATTN_TILE
    maps = 2
    assert n_items >= 2
    group = pl.program_id(2)
    nt = (((1,), (1,)), ((), ()))

    lam_init = lam_init_ref[0]
    lam = lam_ref[...]
    lam_full = (jnp.exp(jnp.sum(lam[0:1] * lam[1:2], axis=-1, keepdims=True))
                - jnp.exp(jnp.sum(lam[2:3] * lam[3:4], axis=-1, keepdims=True)) + lam_init)
    gain = g_ref[...] * (1.0 - lam_init)

    def rows_of(i):
        return pl.ds(i * tq, tq) if isinstance(i, int) else pl.ds(pl.multiple_of(i * tq, tq), tq)

    def qk_cols(m):
        return slice(m * HEAD_DIM, (m + 1) * HEAD_DIM)

    def half_max(x):
        return jnp.maximum(x[:, :LANES], x[:, LANES:])

    def half_sum(x):
        return x[:, :LANES] + x[:, LANES:]

    def run(a_item, b_item):
        if a_item is not None:
            a_row = group * n_items + a_item
            a_q = [q_ref[rows_of(a_item), qk_cols(m)] for m in range(maps)]
            a_max = [None] * maps
        if b_item is not None:
            b_m_wide = [_lane_wide(mx_ref[m], ATTN_TILE) for m in range(maps)]
            b_sum = [None] * maps
            b_acc = [None] * maps

        for u in range(sub):
            cols = slice(u * ATTN_TILE, (u + 1) * ATTN_TILE)
            if b_item is not None:
                vu = v_ref[cols, :]
                for m in range(maps):
                    p = jnp.exp2(s_ref[m, :, cols] - b_m_wide[m])
                    b_sum[m] = half_sum(p) if b_sum[m] is None else b_sum[m] + half_sum(p)
                    pv = jnp.dot(p.astype(BF16), vu, preferred_element_type=F32)
                    b_acc[m] = pv if b_acc[m] is None else pv + b_acc[m]
            if a_item is not None:
                tile = band_ref[jnp.clip(u - a_row, -2, 2) + 2]
                for m in range(maps):
                    s = lax.dot_general(a_q[m], k_ref[cols, qk_cols(m)], nt, preferred_element_type=F32) + tile
                    s_ref[m, :, cols] = s
                    a_max[m] = half_max(s) if a_max[m] is None else jnp.maximum(a_max[m], half_max(s))

        if a_item is not None:
            for m in range(maps):
                mx_ref[m] = jnp.broadcast_to(jnp.max(a_max[m], axis=-1, keepdims=True), (tq, LANES))
        if b_item is not None:
            o1 = b_acc[0] / jnp.sum(b_sum[0], axis=-1, keepdims=True)
            o2 = b_acc[1] / jnp.sum(b_sum[1], axis=-1, keepdims=True)
            of = o1 - lam_full * o2
            of = of * lax.rsqrt(jnp.mean(of * of, axis=-1, keepdims=True) + LN_EPS)
            o_ref[rows_of(b_item), :] = (of * gain).astype(o_ref.dtype)

    run(0, None)

    def body(t, carry):
        run(t + 1, t)
        return carry

    lax.fori_loop(0, n_items - 1, body, 0)
    run(None, n_items - 1)


def _diff_attention(qkvf, lam_l, subln_g_l, band, batch, seq, lam_init):
    tq = _tile(seq, ATTN_ROWS_PER_STEP)
    assert tq % ATTN_TILE == 0 and seq % tq == 0 and seq % ATTN_TILE == 0
    qkv3 = qkvf.reshape(batch, seq, qkvf.shape[-1])
    kern = functools.partial(_attn_kernel, n_items=tq // ATTN_TILE, n_keys=seq)
    out = pl.pallas_call(
        kern,
        grid=(batch, N_HEADS, seq // tq),
        in_specs=[
            pl.BlockSpec(memory_space=pltpu.SMEM),
            pl.BlockSpec((4, HEAD_DIM), lambda b, h, i: (0, 0)),
            pl.BlockSpec((1, V_DIM), lambda b, h, i: (0, 0)),
            pl.BlockSpec((None, tq, V_DIM), lambda b, h, i: (b, i, h)),
            pl.BlockSpec((None, seq, V_DIM), lambda b, h, i: (b, 0, N_HEADS + h)),
            pl.BlockSpec((None, seq, V_DIM), lambda b, h, i: (b, 0, 2 * N_HEADS + h)),
            pl.BlockSpec((None, N_BIAS_TILES, ATTN_TILE, ATTN_TILE), lambda b, h, i: (h, 0, 0, 0)),
        ],
        out_specs=pl.BlockSpec((None, tq, V_DIM), lambda b, h, i: (b, i, h)),
        out_shape=jax.ShapeDtypeStruct((batch, seq, N_HEADS * V_DIM), BF16),
        scratch_shapes=[
            pltpu.VMEM((2, ATTN_TILE, seq), F32),
            pltpu.VMEM((2, ATTN_TILE, LANES), F32),
        ],
        compiler_params=_params("parallel", "parallel", "arbitrary"),
        name="diff_attention",
    )(jnp.full((1,), lam_init, F32), lam_l, subln_g_l.reshape(1, V_DIM), qkv3, qkv3, qkv3, band)
    return out.reshape(batch * seq, N_HEADS * V_DIM)


FNET_S2 = 128
ROW_TILE = 8
FNET_STAGE1_OUT_BYTES = 8 * 1024 * 1024


def _angles(j, k, n):
    return ((j * k) % n).astype(F32) * (2.0 * math.pi / n)


def _chan_dft_table():
    n = FGROUP_DIM
    ang = _angles(lax.broadcasted_iota(jnp.int32, (n, n), 0), lax.broadcasted_iota(jnp.int32, (n, n), 1), n)
    return (jnp.concatenate([jnp.cos(ang), jnp.sin(ang)], axis=1) * n ** -0.5).astype(BF16)


def _seq_dft_tables(seq):
    s2 = min(FNET_S2, seq)
    s1 = seq // s2
    r = min(ROW_TILE, s1)
    eye1 = jnp.eye(ROW_TILE, dtype=F32)
    eye2 = jnp.eye(r, dtype=F32)
    a1 = _angles(lax.broadcasted_iota(jnp.int32, (s1, s1), 0), lax.broadcasted_iota(jnp.int32, (s1, s1), 1), s1)
    c1, sn1 = jnp.cos(a1), jnp.sin(a1)
    m1 = jnp.concatenate([jnp.concatenate([c1, -sn1], axis=1), jnp.concatenate([-sn1, -c1], axis=1)], axis=0)
    m1 = m1 * s1 ** -0.5
    k1m = (m1[:, None, :, None] * eye1[None, :, None, :]).reshape(2 * s1 * ROW_TILE, 2 * s1 * ROW_TILE)

    k1 = lax.broadcasted_iota(jnp.int32, (s1, s2, s2), 0)
    k2 = lax.broadcasted_iota(jnp.int32, (s1, s2, s2), 1)
    n2 = lax.broadcasted_iota(jnp.int32, (s1, s2, s2), 2)
    a2 = _angles(k1 + s1 * k2, n2, seq)
    g = jnp.stack([jnp.cos(a2), jnp.sin(a2)], axis=0) * s2 ** -0.5
    g = g.reshape(2, s1 // r, r, s2, s2)
    k2m = jnp.einsum("pgjkn,jq->gkjpqn", g, eye2).reshape(s1 // r, s2 * r, 2 * r * s2)
    return k1m.astype(BF16), k2m.astype(BF16)


def _chan_stage1_kernel(cs_ref, m_ref, f_ref, o_ref, *, tiles):
    s1, rows, fw = f_ref.shape
    f = f_ref[...].reshape(s1 * rows, fw)
    cs = cs_ref[...]
    zc, zs = [], []
    for g in range(N_FGROUPS):
        z = jnp.dot(f[:, g * FGROUP_DIM:(g + 1) * FGROUP_DIM], cs, preferred_element_type=F32)
        zc.append(z[:, :FGROUP_DIM])
        zs.append(z[:, FGROUP_DIM:])
    z = jnp.stack([jnp.concatenate(zc, axis=1), jnp.concatenate(zs, axis=1)], axis=0)
    z = z.reshape(2, s1, tiles, ROW_TILE, fw)
    for t in range(0, tiles, 2):
        ys = []
        for tt in (t, t + 1):
            x = z[:, :, tt].reshape(2 * s1 * ROW_TILE, fw).astype(BF16)
            y = jnp.dot(m_ref[...], x, preferred_element_type=F32)
            ys.append(y.reshape(2, s1, ROW_TILE, fw))
        o_ref[:, :, t * ROW_TILE:(t + 2) * ROW_TILE, :] = jnp.concatenate(ys, axis=2).astype(o_ref.dtype)


def _seq_stage2_kernel(m_ref, x_ref, o_ref):
    width = x_ref.shape[-1]
    x = x_ref[...].reshape(-1, width)
    o_ref[...] = jnp.dot(m_ref[...], x, preferred_element_type=F32).reshape(o_ref.shape)


def _fourier_mix(qkvf, f_col_block, chan_cs, seq_tabs, batch, seq):
    fw = N_FGROUPS * FGROUP_DIM
    k1m, k2m = seq_tabs
    s2 = min(FNET_S2, seq)
    s1 = seq // s2
    r = min(ROW_TILE, s1)
    assert s2 % ROW_TILE == 0 and s1 % r == 0
    n2_rows = max(2 * ROW_TILE, min(s2, FNET_STAGE1_OUT_BYTES // (2 * s1 * fw * 4)))
    assert s2 % n2_rows == 0 and n2_rows % (2 * ROW_TILE) == 0
    tiles = n2_rows // ROW_TILE
    qkv4 = qkvf.reshape(batch, s1, s2, qkvf.shape[-1])
    a = pl.pallas_call(
        functools.partial(_chan_stage1_kernel, tiles=tiles),
        grid=(batch, s2 // n2_rows),
        in_specs=[
            pl.BlockSpec((FGROUP_DIM, 2 * FGROUP_DIM), lambda b, t: (0, 0)),
            pl.BlockSpec(k1m.shape, lambda b, t: (0, 0)),
            pl.BlockSpec((None, s1, n2_rows, fw), lambda b, t: (b, 0, t, f_col_block)),
        ],
        out_specs=pl.BlockSpec((None, 2, s1, n2_rows, fw), lambda b, t: (b, 0, 0, t, 0)),
        out_shape=jax.ShapeDtypeStruct((batch, 2, s1, s2, fw), BF16),
        compiler_params=_params("parallel", "parallel"),
        name="fnet_chan_dft_stage1",
    )(chan_cs, k1m, qkv4)

    y = pl.pallas_call(
        _seq_stage2_kernel,
        grid=(s1 // r, batch),
        in_specs=[
            pl.BlockSpec((None,) + k2m.shape[1:], lambda g, b: (g, 0, 0)),
            pl.BlockSpec((None, 2, r, s2, fw), lambda g, b: (b, 0, g, 0, 0)),
        ],
        out_specs=pl.BlockSpec((None, s2, r, fw), lambda g, b: (b, 0, g, 0)),
        out_shape=jax.ShapeDtypeStruct((batch, s2, s1, fw), F32),
        compiler_params=_params("parallel", "parallel"),
        name="fnet_seq_dft_stage2",
    )(k2m, a)
    return y.reshape(batch * seq, fw)


def _merge_kernel(o_ref, f_ref, wa_ref, wf_ref, g1_ref, g2_ref, out_ref):
    ya = jnp.dot(o_ref[...], wa_ref[...], preferred_element_type=F32)
    yf = jnp.dot(f_ref[...].astype(BF16), wf_ref[...], preferred_element_type=F32)
    out_ref[...] = (g1_ref[...].astype(F32) * ya + g2_ref[...].astype(F32) * yf).astype(out_ref.dtype)


def _gated_merge(o, fm, gates, w_attn, w_fnet, layer):
    t, ka = o.shape
    kf = fm.shape[1]
    d = w_attn.shape[-1]
    bm = _tile(t, 1024)
    bn = _tile(d, 512)
    nb = d // bn
    return pl.pallas_call(
        _merge_kernel,
        grid=(nb, t // bm),
        in_specs=[
            pl.BlockSpec((bm, ka), lambda j, i: (i, 0)),
            pl.BlockSpec((bm, kf), lambda j, i: (i, 0)),
            pl.BlockSpec((None, ka, bn), lambda j, i: (layer, 0, j)),
            pl.BlockSpec((None, kf, bn), lambda j, i: (layer, 0, j)),
            pl.BlockSpec((bm, bn), lambda j, i: (i, j)),
            pl.BlockSpec((bm, bn), lambda j, i: (i, nb + j)),
        ],
        out_specs=pl.BlockSpec((bm, bn), lambda j, i: (i, j)),
        out_shape=jax.ShapeDtypeStruct((t, d), BF16),
        compiler_params=_params("parallel", "parallel"),
        name="gated_merge",
    )(o, fm, w_attn, w_fnet, gates, gates)


def _proj_res_ln_kernel(a_ref, w_ref, x_ref, g_ref, b_ref, of_ref, ob_ref, acc_ref, *, alpha):
    k = pl.program_id(1)

    @pl.when(k == 0)
    def _():
        acc_ref[...] = jnp.zeros_like(acc_ref)

    acc_ref[...] += jnp.dot(a_ref[...], w_ref[...], preferred_element_type=F32)

    @pl.when(k == pl.num_programs(1) - 1)
    def _():
        y = _layernorm_rows(alpha * x_ref[...] + acc_ref[...], g_ref[...], b_ref[...])
        of_ref[...] = y
        ob_ref[...] = y.astype(BF16)


def _proj_residual_layernorm(a, w, layer, xf, g, b, alpha, bk_pref):
    t, kdim = a.shape
    d = w.shape[-1]
    bm = _tile(t, 512)
    bk = _tile(kdim, bk_pref)
    row = pl.BlockSpec((bm, d), lambda i, k: (i, 0))
    vec = pl.BlockSpec((None, 1, d), lambda i, k: (layer, 0, 0))
    return pl.pallas_call(
        functools.partial(_proj_res_ln_kernel, alpha=alpha),
        grid=(t // bm, kdim // bk),
        in_specs=[
            pl.BlockSpec((bm, bk), lambda i, k: (i, k)),
            pl.BlockSpec((None, bk, d), lambda i, k: (layer, k, 0)),
            row, vec, vec,
        ],
        out_specs=[row, row],
        out_shape=[jax.ShapeDtypeStruct((t, d), F32), jax.ShapeDtypeStruct((t, d), BF16)],
        scratch_shapes=[pltpu.VMEM((bm, d), F32)],
        compiler_params=_params("parallel", "arbitrary"),
        name="proj_residual_layernorm",
    )(a, w, xf, g, b)


def _swiglu_kernel(x_ref, wg_ref, wu_ref, o_ref):
    x = x_ref[...]
    gate = jnp.dot(x, wg_ref[...], preferred_element_type=F32)
    up = jnp.dot(x, wu_ref[...], preferred_element_type=F32)
    o_ref[...] = (gate * jax.nn.sigmoid(gate) * up).astype(o_ref.dtype)


def _swiglu_up(xb, w_gu, layer, d_ff):
    t, d = xb.shape
    bm = _tile(t, 1024)
    bn = _tile(d_ff, 512)
    nb = d_ff // bn
    return pl.pallas_call(
        _swiglu_kernel,
        grid=(nb, t // bm),
        in_specs=[
            pl.BlockSpec((bm, d), lambda j, i: (i, 0)),
            pl.BlockSpec((None, d, bn), lambda j, i: (layer, 0, j)),
            pl.BlockSpec((None, d, bn), lambda j, i: (layer, 0, nb + j)),
        ],
        out_specs=pl.BlockSpec((bm, bn), lambda j, i: (i, j)),
        out_shape=jax.ShapeDtypeStruct((t, d_ff), BF16),
        compiler_params=_params("parallel", "parallel"),
        name="swiglu_up",
    )(xb, w_gu, w_gu)


def _trunk(x, band, chan_cs, seq_tabs, p):
    batch, seq, d = x.shape
    depth = p["w_in"].shape[0]
    alpha = (2.0 * depth) ** 0.25
    qk_width = N_HEADS * 2 * HEAD_DIM
    attn_width = N_HEADS * V_DIM
    fnet_width = N_FGROUPS * FGROUP_DIM
    n_main = 2 * qk_width + attn_width + fnet_width
    assert (2 * qk_width + attn_width) % fnet_width == 0
    f_col_block = (2 * qk_width + attn_width) // fnet_width
    d_ff = p["w_down"].shape[1]

    col_scale = jnp.concatenate(
        [jnp.full((qk_width,), HEAD_DIM ** -0.5 * LOG2_E, F32), jnp.ones((n_main - qk_width,), F32)]).reshape(1, n_main)

    xf, xb = _input_layernorm(x.reshape(batch * seq, d), p["ln_in_g"], p["ln_in_b"])
    for l in range(depth):
        lam_init = 0.8 - 0.6 * math.exp(-0.3 * l)
        qkvf, gates = _in_projection(xb, p["w_in"], l, col_scale, p["b_gate"], n_main)
        o = _diff_attention(qkvf, p["lam"][l], p["subln_g"][l], band, batch, seq, lam_init)
        fm = _fourier_mix(qkvf, f_col_block, chan_cs, seq_tabs, batch, seq)
        merged = _gated_merge(o, fm, gates, p["w_br_attn"], p["w_br_fnet"], l)
        xf, xb = _proj_residual_layernorm(merged, p["w_out"], l, xf, p["ln1_g"], p["ln1_b"], alpha, 2048)
        act = _swiglu_up(xb, p["w_gu"], l, d_ff)
        xf, xb = _proj_residual_layernorm(act, p["w_down"], l, xf, p["ln2_g"], p["ln2_b"], alpha, 2816)
    return xf.reshape(batch, seq, d)


def kernel(x_prompt, x_sample, rel_bias, ln_in_g, ln_in_b, w_in, b_gate, lam, subln_g, w_br_attn, w_br_fnet,
           w_out, ln1_g, ln1_b, w_gu, w_down, ln2_g, ln2_b):
    depth, d = ln1_g.shape
    p = {
        "ln_in_g": ln_in_g, "ln_in_b": ln_in_b,
        "w_in": w_in.astype(BF16), "b_gate": b_gate.reshape(depth, 1, -1),
        "lam": lam, "subln_g": subln_g,
        "w_br_attn": w_br_attn.astype(BF16), "w_br_fnet": w_br_fnet.astype(BF16),
        "w_out": w_out.astype(BF16), "w_gu": w_gu.astype(BF16), "w_down": w_down.astype(BF16),
        "ln1_g": ln1_g.reshape(depth, 1, d), "ln1_b": ln1_b.reshape(depth, 1, d),
        "ln2_g": ln2_g.reshape(depth, 1, d), "ln2_b": ln2_b.reshape(depth, 1, d),
    }
    band = _bias_tiles(rel_bias)
    chan_cs = _chan_dft_table()
    return tuple(_trunk(x, band, chan_cs, _seq_dft_tables(x.shape[1]), p) for x in (x_prompt, x_sample))
```

```python
import functools
import math

import jax
import jax.numpy as jnp
from jax import lax
from jax.experimental import pallas as pl
from jax.experimental.pallas import tpu as pltpu

N_HEADS = 8
HEAD_DIM = 128
V_DIM = 2 * HEAD_DIM
N_FGROUPS = 8
FGROUP_DIM = 128
N_BUCKETS = 32
MAX_DISTANCE = 128
LN_EPS = 1e-5
LOG2_E = math.log2(math.e)

LANES = 128
ATTN_TILE = 256
ATTN_ROWS_PER_STEP = 2048
VMEM_LIMIT_BYTES = 56 * 1024 * 1024

F32 = jnp.float32
BF16 = jnp.bfloat16


def _tile(n, pref):
    if n <= pref:
        return n
    t = (pref // LANES) * LANES
    while t >= LANES:
        if n % t == 0:
            return t
        t -= LANES
    raise ValueError(f"no lane-aligned tile for {n}")


def _params(*sem):
    return pltpu.CompilerParams(dimension_semantics=sem, vmem_limit_bytes=VMEM_LIMIT_BYTES)


def _layernorm_rows(y, g, b):
    mu = jnp.mean(y, axis=-1, keepdims=True)
    yc = y - mu
    var = jnp.mean(yc * yc, axis=-1, keepdims=True)
    return yc * lax.rsqrt(var + LN_EPS) * g + b


def _ln_kernel(x_ref, g_ref, b_ref, of_ref, ob_ref):
    y = _layernorm_rows(x_ref[...], g_ref[...], b_ref[...])
    of_ref[...] = y
    ob_ref[...] = y.astype(BF16)


def _input_layernorm(x, g, b):
    t, d = x.shape
    bm = _tile(t, 1024)
    row = pl.BlockSpec((bm, d), lambda i: (i, 0))
    vec = pl.BlockSpec((1, d), lambda i: (0, 0))
    return pl.pallas_call(
        _ln_kernel,
        grid=(t // bm,),
        in_specs=[row, vec, vec],
        out_specs=[row, row],
        out_shape=[jax.ShapeDtypeStruct((t, d), F32), jax.ShapeDtypeStruct((t, d), BF16)],
        compiler_params=_params("parallel"),
        name="input_layernorm",
    )(x, g.reshape(1, d), b.reshape(1, d))


def _proj_kernel(x_ref, w_ref, s_ref, o_ref):
    acc = jnp.dot(x_ref[...], w_ref[...], preferred_element_type=F32)
    o_ref[...] = (acc * s_ref[...]).astype(o_ref.dtype)


def _gate_kernel(x_ref, w_ref, b_ref, o_ref):
    acc = jnp.dot(x_ref[...], w_ref[...], preferred_element_type=F32)
    o_ref[...] = jax.nn.sigmoid(acc + b_ref[...]).astype(o_ref.dtype)


def _in_projection(xb, w_in, layer, col_scale, b_gate, n_main):
    t, d = xb.shape
    n_gate = w_in.shape[-1] - n_main
    bm = _tile(t, 1024)

    bn = _tile(n_main, 1024)
    qkvf = pl.pallas_call(
        _proj_kernel,
        grid=(n_main // bn, t // bm),
        in_specs=[
            pl.BlockSpec((bm, d), lambda j, i: (i, 0)),
            pl.BlockSpec((None, d, bn), lambda j, i: (layer, 0, j)),
            pl.BlockSpec((1, bn), lambda j, i: (0, j)),
        ],
        out_specs=pl.BlockSpec((bm, bn), lambda j, i: (i, j)),
        out_shape=jax.ShapeDtypeStruct((t, n_main), BF16),
        compiler_params=_params("parallel", "parallel"),
        name="in_proj_qkvf",
    )(xb, w_in, col_scale)

    bg = _tile(n_gate, 1024)
    assert n_main % bg == 0
    off = n_main // bg
    gates = pl.pallas_call(
        _gate_kernel,
        grid=(n_gate // bg, t // bm),
        in_specs=[
            pl.BlockSpec((bm, d), lambda j, i: (i, 0)),
            pl.BlockSpec((None, d, bg), lambda j, i: (layer, 0, off + j)),
            pl.BlockSpec((None, 1, bg), lambda j, i: (layer, 0, j)),
        ],
        out_specs=pl.BlockSpec((bm, bg), lambda j, i: (i, j)),
        out_shape=jax.ShapeDtypeStruct((t, n_gate), BF16),
        compiler_params=_params("parallel", "parallel"),
        name="in_proj_gates",
    )(xb, w_in, b_gate)
    return qkvf, gates


def _t5_bucket(rel):
    nb = N_BUCKETS // 2
    ret = jnp.where(rel > 0, nb, 0)
    n = jnp.abs(rel)
    max_exact = nb // 2
    nf = jnp.maximum(n, 1).astype(F32)
    large = max_exact + (jnp.log(nf / max_exact) / math.log(MAX_DISTANCE / max_exact) * (nb - max_exact)).astype(jnp.int32)
    large = jnp.minimum(large, nb - 1)
    return ret + jnp.where(n < max_exact, n, large)


N_BIAS_TILES = 5


def _bias_tile_kernel(tab_ref, ids_ref, o_ref):
    h = pl.program_id(0)
    ids = ids_ref[...]
    acc = jnp.zeros(ids.shape, F32)
    for b in range(N_BUCKETS):
        acc = jnp.where(ids == b, tab_ref[b, h], acc)
    o_ref[...] = acc * LOG2_E


def _bias_tiles(rel_bias):
    tb = ATTN_TILE
    assert tb >= MAX_DISTANCE
    i = lax.broadcasted_iota(jnp.int32, (N_BIAS_TILES, tb, tb), 1)
    j = lax.broadcasted_iota(jnp.int32, (N_BIAS_TILES, tb, tb), 2)
    d = lax.broadcasted_iota(jnp.int32, (N_BIAS_TILES, tb, tb), 0) - 2
    ids = _t5_bucket(d * tb + j - i).reshape(N_BIAS_TILES * tb, tb)
    out = pl.pallas_call(
        _bias_tile_kernel,
        grid=(N_HEADS,),
        in_specs=[
            pl.BlockSpec(memory_space=pltpu.SMEM),
            pl.BlockSpec((N_BIAS_TILES * tb, tb), lambda h: (0, 0)),
        ],
        out_specs=pl.BlockSpec((None, N_BIAS_TILES * tb, tb), lambda h: (h, 0, 0)),
        out_shape=jax.ShapeDtypeStruct((N_HEADS, N_BIAS_TILES * tb, tb), F32),
        compiler_params=_params("parallel"),
        name="t5_bias_tiles",
    )(rel_bias, ids)
    return out.reshape(N_HEADS, N_BIAS_TILES, tb, tb)


def _lane_wide(x, width):
    return jnp.concatenate([x] * (width // LANES), axis=1)


def _attn_kernel(lam_init_ref, lam_ref, g_ref, q_ref, k_ref, v_ref, band_ref, o_ref, s_ref, mx_ref,
                 *, n_items, n_keys):
    tq = ATTN_TILE
    sub = n_keys // ATTN_TILE
    maps = 2
    assert n_items >= 2
    group = pl.program_id(2)
    nt = (((1,), (1,)), ((), ()))

    lam_init = lam_init_ref[0]
    lam = lam_ref[...]
    lam_full = (jnp.exp(jnp.sum(lam[0:1] * lam[1:2], axis=-1, keepdims=True))
                - jnp.exp(jnp.sum(lam[2:3] * lam[3:4], axis=-1, keepdims=True)) + lam_init)
    gain = g_ref[...] * (1.0 - lam_init)

    def rows_of(i):
        return pl.ds(i * tq, tq) if isinstance(i, int) else pl.ds(pl.multiple_of(i * tq, tq), tq)

    def qk_cols(m):
        return slice(m * HEAD_DIM, (m + 1) * HEAD_DIM)

    def half_max(x):
        return jnp.maximum(x[:, :LANES], x[:, LANES:])

    def half_sum(x):
        return x[:, :LANES] + x[:, LANES:]

    def run(a_item, b_item):
        if a_item is not None:
            a_row = group * n_items + a_item
            a_q = [q_ref[rows_of(a_item), qk_cols(m)] for m in range(maps)]
            a_max = [None] * maps
        if b_item is not None:
            b_m_wide = [_lane_wide(mx_ref[m], ATTN_TILE) for m in range(maps)]
            b_sum = [None] * maps
            b_acc = [None] * maps

        for u in range(sub):
            cols = slice(u * ATTN_TILE, (u + 1) * ATTN_TILE)
            if b_item is not None:
                vu = v_ref[cols, :]
                for m in range(maps):
                    p = jnp.exp2(s_ref[m, :, cols] - b_m_wide[m])
                    b_sum[m] = half_sum(p) if b_sum[m] is None else b_sum[m] + half_sum(p)
                    pv = jnp.dot(p.astype(BF16), vu, preferred_element_type=F32)
                    b_acc[m] = pv if b_acc[m] is None else pv + b_acc[m]
            if a_item is not None:
                tile = band_ref[jnp.clip(u - a_row, -2, 2) + 2]
                for m in range(maps):
                    s = lax.dot_general(a_q[m], k_ref[cols, qk_cols(m)], nt, preferred_element_type=F32) + tile
                    s_ref[m, :, cols] = s
                    a_max[m] = half_max(s) if a_max[m] is None else jnp.maximum(a_max[m], half_max(s))

        if a_item is not None:
            for m in range(maps):
                mx_ref[m] = jnp.broadcast_to(jnp.max(a_max[m], axis=-1, keepdims=True), (tq, LANES))
        if b_item is not None:
            o1 = b_acc[0] / jnp.sum(b_sum[0], axis=-1, keepdims=True)
            o2 = b_acc[1] / jnp.sum(b_sum[1], axis=-1, keepdims=True)
            of = o1 - lam_full * o2
            of = of * lax.rsqrt(jnp.mean(of * of, axis=-1, keepdims=True) + LN_EPS)
            o_ref[rows_of(b_item), :] = (of * gain).astype(o_ref.dtype)

    run(0, None)

    def body(t, carry):
        run(t + 1, t)
        return carry

    lax.fori_loop(0, n_items - 1, body, 0)
    run(None, n_items - 1)


def _diff_attention(qkvf, lam_l, subln_g_l, band, batch, seq, lam_init):
    tq = _tile(seq, ATTN_ROWS_PER_STEP)
    assert tq % ATTN_TILE == 0 and seq % tq == 0 and seq % ATTN_TILE == 0
    qkv3 = qkvf.reshape(batch, seq, qkvf.shape[-1])
    kern = functools.partial(_attn_kernel, n_items=tq // ATTN_TILE, n_keys=seq)
    out = pl.pallas_call(
        kern,
        grid=(batch, N_HEADS, seq // tq),
        in_specs=[
            pl.BlockSpec(memory_space=pltpu.SMEM),
            pl.BlockSpec((4, HEAD_DIM), lambda b, h, i: (0, 0)),
            pl.BlockSpec((1, V_DIM), lambda b, h, i: (0, 0)),
            pl.BlockSpec((None, tq, V_DIM), lambda b, h, i: (b, i, h)),
            pl.BlockSpec((None, seq, V_DIM), lambda b, h, i: (b, 0, N_HEADS + h)),
            pl.BlockSpec((None, seq, V_DIM), lambda b, h, i: (b, 0, 2 * N_HEADS + h)),
            pl.BlockSpec((None, N_BIAS_TILES, ATTN_TILE, ATTN_TILE), lambda b, h, i: (h, 0, 0, 0)),
        ],
        out_specs=pl.BlockSpec((None, tq, V_DIM), lambda b, h, i: (b, i, h)),
        out_shape=jax.ShapeDtypeStruct((batch, seq, N_HEADS * V_DIM), BF16),
        scratch_shapes=[
            pltpu.VMEM((2, ATTN_TILE, seq), F32),
            pltpu.VMEM((2, ATTN_TILE, LANES), F32),
        ],
        compiler_params=_params("parallel", "parallel", "arbitrary"),
        name="diff_attention",
    )(jnp.full((1,), lam_init, F32), lam_l, subln_g_l.reshape(1, V_DIM), qkv3, qkv3, qkv3, band)
    return out.reshape(batch * seq, N_HEADS * V_DIM)


FNET_S2 = 128
ROW_TILE = 8
FNET_STAGE1_OUT_BYTES = 8 * 1024 * 1024


def _angles(j, k, n):
    return ((j * k) % n).astype(F32) * (2.0 * math.pi / n)


def _chan_dft_table():
    n = FGROUP_DIM
    ang = _angles(lax.broadcasted_iota(jnp.int32, (n, n), 0), lax.broadcasted_iota(jnp.int32, (n, n), 1), n)
    return (jnp.concatenate([jnp.cos(ang), jnp.sin(ang)], axis=1) * n ** -0.5).astype(BF16)


def _seq_dft_tables(seq):
    s2 = min(FNET_S2, seq)
    s1 = seq // s2
    r = min(ROW_TILE, s1)
    eye1 = jnp.eye(ROW_TILE, dtype=F32)
    eye2 = jnp.eye(r, dtype=F32)
    a1 = _angles(lax.broadcasted_iota(jnp.int32, (s1, s1), 0), lax.broadcasted_iota(jnp.int32, (s1, s1), 1), s1)
    c1, sn1 = jnp.cos(a1), jnp.sin(a1)
    m1 = jnp.concatenate([jnp.concatenate([c1, -sn1], axis=1), jnp.concatenate([-sn1, -c1], axis=1)], axis=0)
    m1 = m1 * s1 ** -0.5
    k1m = (m1[:, None, :, None] * eye1[None, :, None, :]).reshape(2 * s1 * ROW_TILE, 2 * s1 * ROW_TILE)

    k1 = lax.broadcasted_iota(jnp.int32, (s1, s2, s2), 0)
    k2 = lax.broadcasted_iota(jnp.int32, (s1, s2, s2), 1)
    n2 = lax.broadcasted_iota(jnp.int32, (s1, s2, s2), 2)
    a2 = _angles(k1 + s1 * k2, n2, seq)
    g = jnp.stack([jnp.cos(a2), jnp.sin(a2)], axis=0) * s2 ** -0.5
    g = g.reshape(2, s1 // r, r, s2, s2)
    k2m = jnp.einsum("pgjkn,jq->gkjpqn", g, eye2).reshape(s1 // r, s2 * r, 2 * r * s2)
    return k1m.astype(BF16), k2m.astype(BF16)


def _chan_stage1_kernel(cs_ref, m_ref, f_ref, o_ref, *, tiles):
    s1, rows, fw = f_ref.shape
    f = f_ref[...].reshape(s1 * rows, fw)
    cs = cs_ref[...]
    zc, zs = [], []
    for g in range(N_FGROUPS):
        z = jnp.dot(f[:, g * FGROUP_DIM:(g + 1) * FGROUP_DIM], cs, preferred_element_type=F32)
        zc.append(z[:, :FGROUP_DIM])
        zs.append(z[:, FGROUP_DIM:])
    z = jnp.stack([jnp.concatenate(zc, axis=1), jnp.concatenate(zs, axis=1)], axis=0)
    z = z.reshape(2, s1, tiles, ROW_TILE, fw)
    for t in range(0, tiles, 2):
        ys = []
        for tt in (t, t + 1):
            x = z[:, :, tt].reshape(2 * s1 * ROW_TILE, fw).astype(BF16)
            y = jnp.dot(m_ref[...], x, preferred_element_type=F32)
            ys.append(y.reshape(2, s1, ROW_TILE, fw))
        o_ref[:, :, t * ROW_TILE:(t + 2) * ROW_TILE, :] = jnp.concatenate(ys, axis=2).astype(o_ref.dtype)


def _seq_stage2_kernel(m_ref, x_ref, o_ref):
    width = x_ref.shape[-1]
    x = x_ref[...].reshape(-1, width)
    o_ref[...] = jnp.dot(m_ref[...], x, preferred_element_type=F32).reshape(o_ref.shape)


def _fourier_mix(qkvf, f_col_block, chan_cs, seq_tabs, batch, seq):
    fw = N_FGROUPS * FGROUP_DIM
    k1m, k2m = seq_tabs
    s2 = min(FNET_S2, seq)
    s1 = seq // s2
    r = min(ROW_TILE, s1)
    assert s2 % ROW_TILE == 0 and s1 % r == 0
    n2_rows = max(2 * ROW_TILE, min(s2, FNET_STAGE1_OUT_BYTES // (2 * s1 * fw * 4)))
    assert s2 % n2_rows == 0 and n2_rows % (2 * ROW_TILE) == 0
    tiles = n2_rows // ROW_TILE
    qkv4 = qkvf.reshape(batch, s1, s2, qkvf.shape[-1])
    a = pl.pallas_call(
        functools.partial(_chan_stage1_kernel, tiles=tiles),
        grid=(batch, s2 // n2_rows),
        in_specs=[
            pl.BlockSpec((FGROUP_DIM, 2 * FGROUP_DIM), lambda b, t: (0, 0)),
            pl.BlockSpec(k1m.shape, lambda b, t: (0, 0)),
            pl.BlockSpec((None, s1, n2_rows, fw), lambda b, t: (b, 0, t, f_col_block)),
        ],
        out_specs=pl.BlockSpec((None, 2, s1, n2_rows, fw), lambda b, t: (b, 0, 0, t, 0)),
        out_shape=jax.ShapeDtypeStruct((batch, 2, s1, s2, fw), BF16),
        compiler_params=_params("parallel", "parallel"),
        name="fnet_chan_dft_stage1",
    )(chan_cs, k1m, qkv4)

    y = pl.pallas_call(
        _seq_stage2_kernel,
        grid=(s1 // r, batch),
        in_specs=[
            pl.BlockSpec((None,) + k2m.shape[1:], lambda g, b: (g, 0, 0)),
            pl.BlockSpec((None, 2, r, s2, fw), lambda g, b: (b, 0, g, 0, 0)),
        ],
        out_specs=pl.BlockSpec((None, s2, r, fw), lambda g, b: (b, 0, g, 0)),
        out_shape=jax.ShapeDtypeStruct((batch, s2, s1, fw), F32),
        compiler_params=_params("parallel", "parallel"),
        name="fnet_seq_dft_stage2",
    )(k2m, a)
    return y.reshape(batch * seq, fw)


def _merge_kernel(o_ref, f_ref, wa_ref, wf_ref, g1_ref, g2_ref, out_ref):
    ya = jnp.dot(o_ref[...], wa_ref[...], preferred_element_type=F32)
    yf = jnp.dot(f_ref[...].astype(BF16), wf_ref[...], preferred_element_type=F32)
    out_ref[...] = (g1_ref[...].astype(F32) * ya + g2_ref[...].astype(F32) * yf).astype(out_ref.dtype)


def _gated_merge(o, fm, gates, w_attn, w_fnet, layer):
    t, ka = o.shape
    kf = fm.shape[1]
    d = w_attn.shape[-1]
    bm = _tile(t, 1024)
    bn = _tile(d, 1024)
    nb = d // bn
    return pl.pallas_call(
        _merge_kernel,
        grid=(nb, t // bm),
        in_specs=[
            pl.BlockSpec((bm, ka), lambda j, i: (i, 0)),
            pl.BlockSpec((bm, kf), lambda j, i: (i, 0)),
            pl.BlockSpec((None, ka, bn), lambda j, i: (layer, 0, j)),
            pl.BlockSpec((None, kf, bn), lambda j, i: (layer, 0, j)),
            pl.BlockSpec((bm, bn), lambda j, i: (i, j)),
            pl.BlockSpec((bm, bn), lambda j, i: (i, nb + j)),
        ],
        out_specs=pl.BlockSpec((bm, bn), lambda j, i: (i, j)),
        out_shape=jax.ShapeDtypeStruct((t, d), BF16),
        compiler_params=_params("parallel", "parallel"),
        name="gated_merge",
    )(o, fm, w_attn, w_fnet, gates, gates)


def _proj_res_ln_kernel(a_ref, w_ref, x_ref, g_ref, b_ref, of_ref, ob_ref, acc_ref, *, alpha):
    k = pl.program_id(1)

    @pl.when(k == 0)
    def _():
        acc_ref[...] = jnp.zeros_like(acc_ref)

    acc_ref[...] += jnp.dot(a_ref[...], w_ref[...], preferred_element_type=F32)

    @pl.when(k == pl.num_programs(1) - 1)
    def _():
        y = _layernorm_rows(alpha * x_ref[...] + acc_ref[...], g_ref[...], b_ref[...])
        of_ref[...] = y
        ob_ref[...] = y.astype(BF16)


def _proj_residual_layernorm(a, w, layer, xf, g, b, alpha, bk_pref):
    t, kdim = a.shape
    d = w.shape[-1]
    bm = _tile(t, 512)
    bk = _tile(kdim, bk_pref)
    row = pl.BlockSpec((bm, d), lambda i, k: (i, 0))
    vec = pl.BlockSpec((None, 1, d), lambda i, k: (layer, 0, 0))
    return pl.pallas_call(
        functools.partial(_proj_res_ln_kernel, alpha=alpha),
        grid=(t // bm, kdim // bk),
        in_specs=[
            pl.BlockSpec((bm, bk), lambda i, k: (i, k)),
            pl.BlockSpec((None, bk, d), lambda i, k: (layer, k, 0)),
            row, vec, vec,
        ],
        out_specs=[row, row],
        out_shape=[jax.ShapeDtypeStruct((t, d), F32), jax.ShapeDtypeStruct((t, d), BF16)],
        scratch_shapes=[pltpu.VMEM((bm, d), F32)],
        compiler_params=_params("parallel", "arbitrary"),
        name="proj_residual_layernorm",
    )(a, w, xf, g, b)


def _swiglu_kernel(x_ref, wg_ref, wu_ref, o_ref):
    x = x_ref[...]
    gate = jnp.dot(x, wg_ref[...], preferred_element_type=F32)
    up = jnp.dot(x, wu_ref[...], preferred_element_type=F32)
    o_ref[...] = (gate * jax.nn.sigmoid(gate) * up).astype(o_ref.dtype)


def _swiglu_up(xb, w_gu, layer, d_ff):
    t, d = xb.shape
    bm = _tile(t, 1024)
    bn = _tile(d_ff, 512)
    nb = d_ff // bn
    return pl.pallas_call(
        _swiglu_kernel,
        grid=(nb, t // bm),
        in_specs=[
            pl.BlockSpec((bm, d), lambda j, i: (i, 0)),
            pl.BlockSpec((None, d, bn), lambda j, i: (layer, 0, j)),
            pl.BlockSpec((None, d, bn), lambda j, i: (layer, 0, nb + j)),
        ],
        out_specs=pl.BlockSpec((bm, bn), lambda j, i: (i, j)),
        out_shape=jax.ShapeDtypeStruct((t, d_ff), BF16),
        compiler_params=_params("parallel", "parallel"),
        name="swiglu_up",
    )(xb, w_gu, w_gu)


def _trunk(x, band, chan_cs, seq_tabs, p):
    batch, seq, d = x.shape
    depth = p["w_in"].shape[0]
    alpha = (2.0 * depth) ** 0.25
    qk_width = N_HEADS * 2 * HEAD_DIM
    attn_width = N_HEADS * V_DIM
    fnet_width = N_FGROUPS * FGROUP_DIM
    n_main = 2 * qk_width + attn_width + fnet_width
    assert (2 * qk_width + attn_width) % fnet_width == 0
    f_col_block = (2 * qk_width + attn_width) // fnet_width
    d_ff = p["w_down"].shape[1]

    col_scale = jnp.concatenate(
        [jnp.full((qk_width,), HEAD_DIM ** -0.5 * LOG2_E, F32), jnp.ones((n_main - qk_width,), F32)]).reshape(1, n_main)

    xf, xb = _input_layernorm(x.reshape(batch * seq, d), p["ln_in_g"], p["ln_in_b"])
    for l in range(depth):
        lam_init = 0.8 - 0.6 * math.exp(-0.3 * l)
        qkvf, gates = _in_projection(xb, p["w_in"], l, col_scale, p["b_gate"], n_main)
        o = _diff_attention(qkvf, p["lam"][l], p["subln_g"][l], band, batch, seq, lam_init)
        fm = _fourier_mix(qkvf, f_col_block, chan_cs, seq_tabs, batch, seq)
        merged = _gated_merge(o, fm, gates, p["w_br_attn"], p["w_br_fnet"], l)
        xf, xb = _proj_residual_layernorm(merged, p["w_out"], l, xf, p["ln1_g"], p["ln1_b"], alpha, 2048)
        act = _swiglu_up(xb, p["w_gu"], l, d_ff)
        xf, xb = _proj_residual_layernorm(act, p["w_down"], l, xf, p["ln2_g"], p["ln2_b"], alpha, 2816)
    return xf.reshape(batch, seq, d)


def kernel(x_prompt, x_sample, rel_bias, ln_in_g, ln_in_b, w_in, b_gate, lam, subln_g, w_br_attn, w_br_fnet,
           w_out, ln1_g, ln1_b, w_gu, w_down, ln2_g, ln2_b):
    depth, d = ln1_g.shape
    p = {
        "ln_in_g": ln_in_g, "ln_in_b": ln_in_b,
        "w_in": w_in.astype(BF16), "b_gate": b_gate.reshape(depth, 1, -1),
        "lam": lam, "subln_g": subln_g,
        "w_br_attn": w_br_attn.astype(BF16), "w_br_fnet": w_br_fnet.astype(BF16),
        "w_out": w_out.astype(BF16), "w_gu": w_gu.astype(BF16), "w_down": w_down.astype(BF16),
        "ln1_g": ln1_g.reshape(depth, 1, d), "ln1_b": ln1_b.reshape(depth, 1, d),
        "ln2_g": ln2_g.reshape(depth, 1, d), "ln2_b": ln2_b.reshape(depth, 1, d),
    }
    band = _bias_tiles(rel_bias)
    chan_cs = _chan_dft_table()
    return tuple(_trunk(x, band, chan_cs, _seq_dft_tables(x.shape[1]), p) for x in (x_prompt, x_sample))
```

```python
import functools
import math

import jax
import jax.numpy as jnp
from jax import lax
from jax.experimental import pallas as pl
from jax.experimental.pallas import tpu as pltpu

N_HEADS = 8
HEAD_DIM = 128
V_DIM = 2 * HEAD_DIM
N_FGROUPS = 8
FGROUP_DIM = 128
N_BUCKETS = 32
MAX_DISTANCE = 128
LN_EPS = 1e-5
LOG2_E = math.log2(math.e)

LANES = 128
ATTN_TILE = 256
ATTN_ROWS_PER_STEP = 4096
VMEM_LIMIT_BYTES = 56 * 1024 * 1024

F32 = jnp.float32
BF16 = jnp.bfloat16


def _tile(n, pref):
    if n <= pref:
        return n
    t = (pref // LANES) * LANES
    while t >= LANES:
        if n % t == 0:
            return t
        t -= LANES
    raise ValueError(f"no lane-aligned tile for {n}")


def _params(*sem):
    return pltpu.CompilerParams(dimension_semantics=sem, vmem_limit_bytes=VMEM_LIMIT_BYTES)


def _layernorm_rows(y, g, b):
    mu = jnp.mean(y, axis=-1, keepdims=True)
    yc = y - mu
    var = jnp.mean(yc * yc, axis=-1, keepdims=True)
    return yc * lax.rsqrt(var + LN_EPS) * g + b


def _ln_kernel(x_ref, g_ref, b_ref, of_ref, ob_ref):
    y = _layernorm_rows(x_ref[...], g_ref[...], b_ref[...])
    of_ref[...] = y
    ob_ref[...] = y.astype(BF16)


def _input_layernorm(x, g, b):
    t, d = x.shape
    bm = _tile(t, 1024)
    row = pl.BlockSpec((bm, d), lambda i: (i, 0))
    vec = pl.BlockSpec((1, d), lambda i: (0, 0))
    return pl.pallas_call(
        _ln_kernel,
        grid=(t // bm,),
        in_specs=[row, vec, vec],
        out_specs=[row, row],
        out_shape=[jax.ShapeDtypeStruct((t, d), F32), jax.ShapeDtypeStruct((t, d), BF16)],
        compiler_params=_params("parallel"),
        name="input_layernorm",
    )(x, g.reshape(1, d), b.reshape(1, d))


def _proj_kernel(x_ref, w_ref, s_ref, o_ref):
    acc = jnp.dot(x_ref[...], w_ref[...], preferred_element_type=F32)
    o_ref[...] = (acc * s_ref[...]).astype(o_ref.dtype)


def _gate_kernel(x_ref, w_ref, b_ref, o_ref):
    acc = jnp.dot(x_ref[...], w_ref[...], preferred_element_type=F32)
    o_ref[...] = jax.nn.sigmoid(acc + b_ref[...]).astype(o_ref.dtype)


def _in_projection(xb, w_in, layer, col_scale, b_gate, n_main):
    t, d = xb.shape
    n_gate = w_in.shape[-1] - n_main
    bm = _tile(t, 1024)

    bn = _tile(n_main, 1792)
    qkvf = pl.pallas_call(
        _proj_kernel,
        grid=(n_main // bn, t // bm),
        in_specs=[
            pl.BlockSpec((bm, d), lambda j, i: (i, 0)),
            pl.BlockSpec((None, d, bn), lambda j, i: (layer, 0, j)),
            pl.BlockSpec((1, bn), lambda j, i: (0, j)),
        ],
        out_specs=pl.BlockSpec((bm, bn), lambda j, i: (i, j)),
        out_shape=jax.ShapeDtypeStruct((t, n_main), BF16),
        compiler_params=_params("parallel", "parallel"),
        name="in_proj_qkvf",
    )(xb, w_in, col_scale)

    bg = _tile(n_gate, 1024)
    assert n_main % bg == 0
    off = n_main // bg
    gates = pl.pallas_call(
        _gate_kernel,
        grid=(n_gate // bg, t // bm),
        in_specs=[
            pl.BlockSpec((bm, d), lambda j, i: (i, 0)),
            pl.BlockSpec((None, d, bg), lambda j, i: (layer, 0, off + j)),
            pl.BlockSpec((None, 1, bg), lambda j, i: (layer, 0, j)),
        ],
        out_specs=pl.BlockSpec((bm, bg), lambda j, i: (i, j)),
        out_shape=jax.ShapeDtypeStruct((t, n_gate), BF16),
        compiler_params=_params("parallel", "parallel"),
        name="in_proj_gates",
    )(xb, w_in, b_gate)
    return qkvf, gates


def _t5_bucket(rel):
    nb = N_BUCKETS // 2
    ret = jnp.where(rel > 0, nb, 0)
    n = jnp.abs(rel)
    max_exact = nb // 2
    nf = jnp.maximum(n, 1).astype(F32)
    large = max_exact + (jnp.log(nf / max_exact) / math.log(MAX_DISTANCE / max_exact) * (nb - max_exact)).astype(jnp.int32)
    large = jnp.minimum(large, nb - 1)
    return ret + jnp.where(n < max_exact, n, large)


N_BIAS_TILES = 5


def _bias_tile_kernel(tab_ref, ids_ref, o_ref):
    h = pl.program_id(0)
    ids = ids_ref[...]
    acc = jnp.zeros(ids.shape, F32)
    for b in range(N_BUCKETS):
        acc = jnp.where(ids == b, tab_ref[b, h], acc)
    o_ref[...] = acc * LOG2_E


def _bias_tiles(rel_bias):
    tb = ATTN_TILE
    assert tb >= MAX_DISTANCE
    i = lax.broadcasted_iota(jnp.int32, (N_BIAS_TILES, tb, tb), 1)
    j = lax.broadcasted_iota(jnp.int32, (N_BIAS_TILES, tb, tb), 2)
    d = lax.broadcasted_iota(jnp.int32, (N_BIAS_TILES, tb, tb), 0) - 2
    ids = _t5_bucket(d * tb + j - i).reshape(N_BIAS_TILES * tb, tb)
    out = pl.pallas_call(
        _bias_tile_kernel,
        grid=(N_HEADS,),
        in_specs=[
            pl.BlockSpec(memory_space=pltpu.SMEM),
            pl.BlockSpec((N_BIAS_TILES * tb, tb), lambda h: (0, 0)),
        ],
        out_specs=pl.BlockSpec((None, N_BIAS_TILES * tb, tb), lambda h: (h, 0, 0)),
        out_shape=jax.ShapeDtypeStruct((N_HEADS, N_BIAS_TILES * tb, tb), F32),
        compiler_params=_params("parallel"),
        name="t5_bias_tiles",
    )(rel_bias, ids)
    return out.reshape(N_HEADS, N_BIAS_TILES, tb, tb)


def _lane_wide(x, width):
    return jnp.concatenate([x] * (width // LANES), axis=1)


def _attn_kernel(lam_init_ref, lam_ref, g_ref, q_ref, k_ref, v_ref, band_ref, o_ref, s_ref, mx_ref,
                 *, n_items, n_keys):
    tq = ATTN_TILE
    sub = n_keys // ATTN_TILE
    maps = 2
    assert n_items >= 2
    group = pl.program_id(2)
    nt = (((1,), (1,)), ((), ()))

    lam_init = lam_init_ref[0]
    lam = lam_ref[...]
    lam_full = (jnp.exp(jnp.sum(lam[0:1] * lam[1:2], axis=-1, keepdims=True))
                - jnp.exp(jnp.sum(lam[2:3] * lam[3:4], axis=-1, keepdims=True)) + lam_init)
    gain = g_ref[...] * (1.0 - lam_init)

    def rows_of(i):
        return pl.ds(i * tq, tq) if isinstance(i, int) else pl.ds(pl.multiple_of(i * tq, tq), tq)

    def qk_cols(m):
        return slice(m * HEAD_DIM, (m + 1) * HEAD_DIM)

    def half_max(x):
        return jnp.maximum(x[:, :LANES], x[:, LANES:])

    def half_sum(x):
        return x[:, :LANES] + x[:, LANES:]

    def run(a_item, b_item):
        if a_item is not None:
            a_row = group * n_items + a_item
            a_q = [q_ref[rows_of(a_item), qk_cols(m)] for m in range(maps)]
            a_max = [None] * maps
        if b_item is not None:
            b_m_wide = [_lane_wide(mx_ref[m], ATTN_TILE) for m in range(maps)]
            b_sum = [None] * maps
            b_acc = [None] * maps

        for u in range(sub):
            cols = slice(u * ATTN_TILE, (u + 1) * ATTN_TILE)
            if b_item is not None:
                vu = v_ref[cols, :]
                for m in range(maps):
                    p = jnp.exp2(s_ref[m, :, cols] - b_m_wide[m])
                    b_sum[m] = half_sum(p) if b_sum[m] is None else b_sum[m] + half_sum(p)
                    pv = jnp.dot(p.astype(BF16), vu, preferred_element_type=F32)
                    b_acc[m] = pv if b_acc[m] is None else pv + b_acc[m]
            if a_item is not None:
                tile = band_ref[jnp.clip(u - a_row, -2, 2) + 2]
                for m in range(maps):
                    s = lax.dot_general(a_q[m], k_ref[cols, qk_cols(m)], nt, preferred_element_type=F32) + tile
                    s_ref[m, :, cols] = s
                    a_max[m] = half_max(s) if a_max[m] is None else jnp.maximum(a_max[m], half_max(s))

        if a_item is not None:
            for m in range(maps):
                mx_ref[m] = jnp.broadcast_to(jnp.max(a_max[m], axis=-1, keepdims=True), (tq, LANES))
        if b_item is not None:
            o1 = b_acc[0] / jnp.sum(b_sum[0], axis=-1, keepdims=True)
            o2 = b_acc[1] / jnp.sum(b_sum[1], axis=-1, keepdims=True)
            of = o1 - lam_full * o2
            of = of * lax.rsqrt(jnp.mean(of * of, axis=-1, keepdims=True) + LN_EPS)
            o_ref[rows_of(b_item), :] = (of * gain).astype(o_ref.dtype)

    run(0, None)

    def body(t, carry):
        run(t + 1, t)
        return carry

    lax.fori_loop(0, n_items - 1, body, 0)
    run(None, n_items - 1)


def _diff_attention(qkvf, lam_l, subln_g_l, band, batch, seq, lam_init):
    tq = _tile(seq, ATTN_ROWS_PER_STEP)
    assert tq % ATTN_TILE == 0 and seq % tq == 0 and seq % ATTN_TILE == 0
    qkv3 = qkvf.reshape(batch, seq, qkvf.shape[-1])
    kern = functools.partial(_attn_kernel, n_items=tq // ATTN_TILE, n_keys=seq)
    out = pl.pallas_call(
        kern,
        grid=(batch, N_HEADS, seq // tq),
        in_specs=[
            pl.BlockSpec(memory_space=pltpu.SMEM),
            pl.BlockSpec((4, HEAD_DIM), lambda b, h, i: (0, 0)),
            pl.BlockSpec((1, V_DIM), lambda b, h, i: (0, 0)),
            pl.BlockSpec((None, tq, V_DIM), lambda b, h, i: (b, i, h)),
            pl.BlockSpec((None, seq, V_DIM), lambda b, h, i: (b, 0, N_HEADS + h)),
            pl.BlockSpec((None, seq, V_DIM), lambda b, h, i: (b, 0, 2 * N_HEADS + h)),
            pl.BlockSpec((None, N_BIAS_TILES, ATTN_TILE, ATTN_TILE), lambda b, h, i: (h, 0, 0, 0)),
        ],
        out_specs=pl.BlockSpec((None, tq, V_DIM), lambda b, h, i: (b, i, h)),
        out_shape=jax.ShapeDtypeStruct((batch, seq, N_HEADS * V_DIM), BF16),
        scratch_shapes=[
            pltpu.VMEM((2, ATTN_TILE, seq), F32),
            pltpu.VMEM((2, ATTN_TILE, LANES), F32),
        ],
        compiler_params=_params("parallel", "parallel", "arbitrary"),
        name="diff_attention",
    )(jnp.full((1,), lam_init, F32), lam_l, subln_g_l.reshape(1, V_DIM), qkv3, qkv3, qkv3, band)
    return out.reshape(batch * seq, N_HEADS * V_DIM)


FNET_S2 = 128
ROW_TILE = 8
FNET_STAGE1_OUT_BYTES = 8 * 1024 * 1024


def _angles(j, k, n):
    return ((j * k) % n).astype(F32) * (2.0 * math.pi / n)


def _chan_dft_table():
    n = FGROUP_DIM
    ang = _angles(lax.broadcasted_iota(jnp.int32, (n, n), 0), lax.broadcasted_iota(jnp.int32, (n, n), 1), n)
    return (jnp.concatenate([jnp.cos(ang), jnp.sin(ang)], axis=1) * n ** -0.5).astype(BF16)


def _seq_dft_tables(seq):
    s2 = min(FNET_S2, seq)
    s1 = seq // s2
    r = min(ROW_TILE, s1)
    eye1 = jnp.eye(ROW_TILE, dtype=F32)
    eye2 = jnp.eye(r, dtype=F32)
    a1 = _angles(lax.broadcasted_iota(jnp.int32, (s1, s1), 0), lax.broadcasted_iota(jnp.int32, (s1, s1), 1), s1)
    c1, sn1 = jnp.cos(a1), jnp.sin(a1)
    m1 = jnp.concatenate([jnp.concatenate([c1, -sn1], axis=1), jnp.concatenate([-sn1, -c1], axis=1)], axis=0)
    m1 = m1 * s1 ** -0.5
    k1m = (m1[:, None, :, None] * eye1[None, :, None, :]).reshape(2 * s1 * ROW_TILE, 2 * s1 * ROW_TILE)

    k1 = lax.broadcasted_iota(jnp.int32, (s1, s2, s2), 0)
    k2 = lax.broadcasted_iota(jnp.int32, (s1, s2, s2), 1)
    n2 = lax.broadcasted_iota(jnp.int32, (s1, s2, s2), 2)
    a2 = _angles(k1 + s1 * k2, n2, seq)
    g = jnp.stack([jnp.cos(a2), jnp.sin(a2)], axis=0) * s2 ** -0.5
    g = g.reshape(2, s1 // r, r, s2, s2)
    k2m = jnp.einsum("pgjkn,jq->gkjpqn", g, eye2).reshape(s1 // r, s2 * r, 2 * r * s2)
    return k1m.astype(BF16), k2m.astype(BF16)


def _chan_stage1_kernel(cs_ref, m_ref, f_ref, o_ref, *, tiles):
    s1, rows, fw = f_ref.shape
    f = f_ref[...].reshape(s1 * rows, fw)
    cs = cs_ref[...]
    zc, zs = [], []
    for g in range(N_FGROUPS):
        z = jnp.dot(f[:, g * FGROUP_DIM:(g + 1) * FGROUP_DIM], cs, preferred_element_type=F32)
        zc.append(z[:, :FGROUP_DIM])
        zs.append(z[:, FGROUP_DIM:])
    z = jnp.stack([jnp.concatenate(zc, axis=1), jnp.concatenate(zs, axis=1)], axis=0)
    z = z.reshape(2, s1, tiles, ROW_TILE, fw)
    for t in range(0, tiles, 2):
        ys = []
        for tt in (t, t + 1):
            x = z[:, :, tt].reshape(2 * s1 * ROW_TILE, fw).astype(BF16)
            y = jnp.dot(m_ref[...], x, preferred_element_type=F32)
            ys.append(y.reshape(2, s1, ROW_TILE, fw))
        o_ref[:, :, t * ROW_TILE:(t + 2) * ROW_TILE, :] = jnp.concatenate(ys, axis=2).astype(o_ref.dtype)


def _seq_stage2_kernel(m_ref, x_ref, o_ref):
    width = x_ref.shape[-1]
    x = x_ref[...].reshape(-1, width)
    o_ref[...] = jnp.dot(m_ref[...], x, preferred_element_type=F32).reshape(o_ref.shape)


def _fourier_mix(qkvf, f_col_block, chan_cs, seq_tabs, batch, seq):
    fw = N_FGROUPS * FGROUP_DIM
    k1m, k2m = seq_tabs
    s2 = min(FNET_S2, seq)
    s1 = seq // s2
    r = min(ROW_TILE, s1)
    assert s2 % ROW_TILE == 0 and s1 % r == 0
    n2_rows = max(2 * ROW_TILE, min(s2, FNET_STAGE1_OUT_BYTES // (2 * s1 * fw * 4)))
    assert s2 % n2_rows == 0 and n2_rows % (2 * ROW_TILE) == 0
    tiles = n2_rows // ROW_TILE
    qkv4 = qkvf.reshape(batch, s1, s2, qkvf.shape[-1])
    a = pl.pallas_call(
        functools.partial(_chan_stage1_kernel, tiles=tiles),
        grid=(batch, s2 // n2_rows),
        in_specs=[
            pl.BlockSpec((FGROUP_DIM, 2 * FGROUP_DIM), lambda b, t: (0, 0)),
            pl.BlockSpec(k1m.shape, lambda b, t: (0, 0)),
            pl.BlockSpec((None, s1, n2_rows, fw), lambda b, t: (b, 0, t, f_col_block)),
        ],
        out_specs=pl.BlockSpec((None, 2, s1, n2_rows, fw), lambda b, t: (b, 0, 0, t, 0)),
        out_shape=jax.ShapeDtypeStruct((batch, 2, s1, s2, fw), BF16),
        compiler_params=_params("parallel", "parallel"),
        name="fnet_chan_dft_stage1",
    )(chan_cs, k1m, qkv4)

    y = pl.pallas_call(
        _seq_stage2_kernel,
        grid=(s1 // r, batch),
        in_specs=[
            pl.BlockSpec((None,) + k2m.shape[1:], lambda g, b: (g, 0, 0)),
            pl.BlockSpec((None, 2, r, s2, fw), lambda g, b: (b, 0, g, 0, 0)),
        ],
        out_specs=pl.BlockSpec((None, s2, r, fw), lambda g, b: (b, 0, g, 0)),
        out_shape=jax.ShapeDtypeStruct((batch, s2, s1, fw), F32),
        compiler_params=_params("parallel", "parallel"),
        name="fnet_seq_dft_stage2",
    )(k2m, a)
    return y.reshape(batch * seq, fw)


def _merge_kernel(o_ref, f_ref, wa_ref, wf_ref, g1_ref, g2_ref, out_ref):
    ya = jnp.dot(o_ref[...], wa_ref[...], preferred_element_type=F32)
    yf = jnp.dot(f_ref[...].astype(BF16), wf_ref[...], preferred_element_type=F32)
    out_ref[...] = (g1_ref[...].astype(F32) * ya + g2_ref[...].astype(F32) * yf).astype(out_ref.dtype)


def _gated_merge(o, fm, gates, w_attn, w_fnet, layer):
    t, ka = o.shape
    kf = fm.shape[1]
    d = w_attn.shape[-1]
    bm = _tile(t, 1024)
    bn = _tile(d, 1024)
    nb = d // bn
    return pl.pallas_call(
        _merge_kernel,
        grid=(nb, t // bm),
        in_specs=[
            pl.BlockSpec((bm, ka), lambda j, i: (i, 0)),
            pl.BlockSpec((bm, kf), lambda j, i: (i, 0)),
            pl.BlockSpec((None, ka, bn), lambda j, i: (layer, 0, j)),
            pl.BlockSpec((None, kf, bn), lambda j, i: (layer, 0, j)),
            pl.BlockSpec((bm, bn), lambda j, i: (i, j)),
            pl.BlockSpec((bm, bn), lambda j, i: (i, nb + j)),
        ],
        out_specs=pl.BlockSpec((bm, bn), lambda j, i: (i, j)),
        out_shape=jax.ShapeDtypeStruct((t, d), BF16),
        compiler_params=_params("parallel", "parallel"),
        name="gated_merge",
    )(o, fm, w_attn, w_fnet, gates, gates)


def _proj_res_ln_kernel(a_ref, w_ref, x_ref, g_ref, b_ref, of_ref, ob_ref, acc_ref, *, alpha):
    k = pl.program_id(1)

    @pl.when(k == 0)
    def _():
        acc_ref[...] = jnp.zeros_like(acc_ref)

    acc_ref[...] += jnp.dot(a_ref[...], w_ref[...], preferred_element_type=F32)

    @pl.when(k == pl.num_programs(1) - 1)
    def _():
        y = _layernorm_rows(alpha * x_ref[...] + acc_ref[...], g_ref[...], b_ref[...])
        of_ref[...] = y
        ob_ref[...] = y.astype(BF16)


def _proj_residual_layernorm(a, w, layer, xf, g, b, alpha, bk_pref):
    t, kdim = a.shape
    d = w.shape[-1]
    bm = _tile(t, 512)
    bk = _tile(kdim, bk_pref)
    row = pl.BlockSpec((bm, d), lambda i, k: (i, 0))
    vec = pl.BlockSpec((None, 1, d), lambda i, k: (layer, 0, 0))
    return pl.pallas_call(
        functools.partial(_proj_res_ln_kernel, alpha=alpha),
        grid=(t // bm, kdim // bk),
        in_specs=[
            pl.BlockSpec((bm, bk), lambda i, k: (i, k)),
            pl.BlockSpec((None, bk, d), lambda i, k: (layer, k, 0)),
            row, vec, vec,
        ],
        out_specs=[row, row],
        out_shape=[jax.ShapeDtypeStruct((t, d), F32), jax.ShapeDtypeStruct((t, d), BF16)],
        scratch_shapes=[pltpu.VMEM((bm, d), F32)],
        compiler_params=_params("parallel", "arbitrary"),
        name="proj_residual_layernorm",
    )(a, w, xf, g, b)


def _swiglu_kernel(x_ref, wg_ref, wu_ref, o_ref):
    x = x_ref[...]
    gate = jnp.dot(x, wg_ref[...], preferred_element_type=F32)
    up = jnp.dot(x, wu_ref[...], preferred_element_type=F32)
    o_ref[...] = (gate * jax.nn.sigmoid(gate) * up).astype(o_ref.dtype)


def _swiglu_up(xb, w_gu, layer, d_ff):
    t, d = xb.shape
    bm = _tile(t, 1024)
    bn = _tile(d_ff, 512)
    nb = d_ff // bn
    return pl.pallas_call(
        _swiglu_kernel,
        grid=(nb, t // bm),
        in_specs=[
            pl.BlockSpec((bm, d), lambda j, i: (i, 0)),
            pl.BlockSpec((None, d, bn), lambda j, i: (layer, 0, j)),
            pl.BlockSpec((None, d, bn), lambda j, i: (layer, 0, nb + j)),
        ],
        out_specs=pl.BlockSpec((bm, bn), lambda j, i: (i, j)),
        out_shape=jax.ShapeDtypeStruct((t, d_ff), BF16),
        compiler_params=_params("parallel", "parallel"),
        name="swiglu_up",
    )(xb, w_gu, w_gu)


def _trunk(x, band, chan_cs, seq_tabs, p):
    batch, seq, d = x.shape
    depth = p["w_in"].shape[0]
    alpha = (2.0 * depth) ** 0.25
    qk_width = N_HEADS * 2 * HEAD_DIM
    attn_width = N_HEADS * V_DIM
    fnet_width = N_FGROUPS * FGROUP_DIM
    n_main = 2 * qk_width + attn_width + fnet_width
    assert (2 * qk_width + attn_width) % fnet_width == 0
    f_col_block = (2 * qk_width + attn_width) // fnet_width
    d_ff = p["w_down"].shape[1]

    col_scale = jnp.concatenate(
        [jnp.full((qk_width,), HEAD_DIM ** -0.5 * LOG2_E, F32), jnp.ones((n_main - qk_width,), F32)]).reshape(1, n_main)

    xf, xb = _input_layernorm(x.reshape(batch * seq, d), p["ln_in_g"], p["ln_in_b"])
    for l in range(depth):
        lam_init = 0.8 - 0.6 * math.exp(-0.3 * l)
        qkvf, gates = _in_projection(xb, p["w_in"], l, col_scale, p["b_gate"], n_main)
        o = _diff_attention(qkvf, p["lam"][l], p["subln_g"][l], band, batch, seq, lam_init)
        fm = _fourier_mix(qkvf, f_col_block, chan_cs, seq_tabs, batch, seq)
        merged = _gated_merge(o, fm, gates, p["w_br_attn"], p["w_br_fnet"], l)
        xf, xb = _proj_residual_layernorm(merged, p["w_out"], l, xf, p["ln1_g"], p["ln1_b"], alpha, 2048)
        act = _swiglu_up(xb, p["w_gu"], l, d_ff)
        xf, xb = _proj_residual_layernorm(act, p["w_down"], l, xf, p["ln2_g"], p["ln2_b"], alpha, 2816)
    return xf.reshape(batch, seq, d)


def kernel(x_prompt, x_sample, rel_bias, ln_in_g, ln_in_b, w_in, b_gate, lam, subln_g, w_br_attn, w_br_fnet,
           w_out, ln1_g, ln1_b, w_gu, w_down, ln2_g, ln2_b):
    depth, d = ln1_g.shape
    p = {
        "ln_in_g": ln_in_g, "ln_in_b": ln_in_b,
        "w_in": w_in.astype(BF16), "b_gate": b_gate.reshape(depth, 1, -1),
        "lam": lam, "subln_g": subln_g,
        "w_br_attn": w_br_attn.astype(BF16), "w_br_fnet": w_br_fnet.astype(BF16),
        "w_out": w_out.astype(BF16), "w_gu": w_gu.astype(BF16), "w_down": w_down.astype(BF16),
        "ln1_g": ln1_g.reshape(depth, 1, d), "ln1_b": ln1_b.reshape(depth, 1, d),
        "ln2_g": ln2_g.reshape(depth, 1, d), "ln2_b": ln2_b.reshape(depth, 1, d),
    }
    band = _bias_tiles(rel_bias)
    chan_cs = _chan_dft_table()
    return tuple(_trunk(x, band, chan_cs, _seq_dft_tables(x.shape[1]), p) for x in (x_prompt, x_sample))
```

```python
import functools
import math

import jax
import jax.numpy as jnp
from jax import lax
from jax.experimental import pallas as pl
from jax.experimental.pallas import tpu as pltpu

N_HEADS = 8
HEAD_DIM = 128
V_DIM = 2 * HEAD_DIM
N_FGROUPS = 8
FGROUP_DIM = 128
N_BUCKETS = 32
MAX_DISTANCE = 128
LN_EPS = 1e-5
LOG2_E = math.log2(math.e)

LANES = 128
ATTN_TILE = 256
ATTN_ROWS_PER_STEP = 2048
VMEM_LIMIT_BYTES = 56 * 1024 * 1024

F32 = jnp.float32
BF16 = jnp.bfloat16


def _tile(n, pref):
    if n <= pref:
        return n
    t = (pref // LANES) * LANES
    while t >= LANES:
        if n % t == 0:
            return t
        t -= LANES
    raise ValueError(f"no lane-aligned tile for {n}")


def _params(*sem):
    return pltpu.CompilerParams(dimension_semantics=sem, vmem_limit_bytes=VMEM_LIMIT_BYTES)


def _layernorm_rows(y, g, b):
    mu = jnp.mean(y, axis=-1, keepdims=True)
    yc = y - mu
    var = jnp.mean(yc * yc, axis=-1, keepdims=True)
    return yc * lax.rsqrt(var + LN_EPS) * g + b


def _ln_kernel(x_ref, g_ref, b_ref, of_ref, ob_ref):
    y = _layernorm_rows(x_ref[...], g_ref[...], b_ref[...])
    of_ref[...] = y
    ob_ref[...] = y.astype(BF16)


def _input_layernorm(x, g, b):
    t, d = x.shape
    bm = _tile(t, 1024)
    row = pl.BlockSpec((bm, d), lambda i: (i, 0))
    vec = pl.BlockSpec((1, d), lambda i: (0, 0))
    return pl.pallas_call(
        _ln_kernel,
        grid=(t // bm,),
        in_specs=[row, vec, vec],
        out_specs=[row, row],
        out_shape=[jax.ShapeDtypeStruct((t, d), F32), jax.ShapeDtypeStruct((t, d), BF16)],
        compiler_params=_params("parallel"),
        name="input_layernorm",
    )(x, g.reshape(1, d), b.reshape(1, d))


def _proj_kernel(x_ref, w_ref, s_ref, o_ref):
    acc = jnp.dot(x_ref[...], w_ref[...], preferred_element_type=F32)
    o_ref[...] = (acc * s_ref[...]).astype(o_ref.dtype)


def _gate_kernel(x_ref, w_ref, b_ref, o_ref):
    acc = jnp.dot(x_ref[...], w_ref[...], preferred_element_type=F32)
    o_ref[...] = jax.nn.sigmoid(acc + b_ref[...]).astype(o_ref.dtype)


def _in_projection(xb, w_in, layer, col_scale, b_gate, n_main):
    t, d = xb.shape
    n_gate = w_in.shape[-1] - n_main
    bm = _tile(t, 1024)

    bn = _tile(n_main, 1792)
    qkvf = pl.pallas_call(
        _proj_kernel,
        grid=(n_main // bn, t // bm),
        in_specs=[
            pl.BlockSpec((bm, d), lambda j, i: (i, 0)),
            pl.BlockSpec((None, d, bn), lambda j, i: (layer, 0, j)),
            pl.BlockSpec((1, bn), lambda j, i: (0, j)),
        ],
        out_specs=pl.BlockSpec((bm, bn), lambda j, i: (i, j)),
        out_shape=jax.ShapeDtypeStruct((t, n_main), BF16),
        compiler_params=_params("parallel", "parallel"),
        name="in_proj_qkvf",
    )(xb, w_in, col_scale)

    bg = _tile(n_gate, 1024)
    assert n_main % bg == 0
    off = n_main // bg
    gates = pl.pallas_call(
        _gate_kernel,
        grid=(n_gate // bg, t // bm),
        in_specs=[
            pl.BlockSpec((bm, d), lambda j, i: (i, 0)),
            pl.BlockSpec((None, d, bg), lambda j, i: (layer, 0, off + j)),
            pl.BlockSpec((None, 1, bg), lambda j, i: (layer, 0, j)),
        ],
        out_specs=pl.BlockSpec((bm, bg), lambda j, i: (i, j)),
        out_shape=jax.ShapeDtypeStruct((t, n_gate), BF16),
        compiler_params=_params("parallel", "parallel"),
        name="in_proj_gates",
    )(xb, w_in, b_gate)
    return qkvf, gates


def _t5_bucket(rel):
    nb = N_BUCKETS // 2
    ret = jnp.where(rel > 0, nb, 0)
    n = jnp.abs(rel)
    max_exact = nb // 2
    nf = jnp.maximum(n, 1).astype(F32)
    large = max_exact + (jnp.log(nf / max_exact) / math.log(MAX_DISTANCE / max_exact) * (nb - max_exact)).astype(jnp.int32)
    large = jnp.minimum(large, nb - 1)
    return ret + jnp.where(n < max_exact, n, large)


N_BIAS_TILES = 5


def _bias_tile_kernel(tab_ref, ids_ref, o_ref):
    h = pl.program_id(0)
    ids = ids_ref[...]
    acc = jnp.zeros(ids.shape, F32)
    for b in range(N_BUCKETS):
        acc = jnp.where(ids == b, tab_ref[b, h], acc)
    o_ref[...] = acc * LOG2_E


def _bias_tiles(rel_bias):
    tb = ATTN_TILE
    assert tb >= MAX_DISTANCE
    i = lax.broadcasted_iota(jnp.int32, (N_BIAS_TILES, tb, tb), 1)
    j = lax.broadcasted_iota(jnp.int32, (N_BIAS_TILES, tb, tb), 2)
    d = lax.broadcasted_iota(jnp.int32, (N_BIAS_TILES, tb, tb), 0) - 2
    ids = _t5_bucket(d * tb + j - i).reshape(N_BIAS_TILES * tb, tb)
    out = pl.pallas_call(
        _bias_tile_kernel,
        grid=(N_HEADS,),
        in_specs=[
            pl.BlockSpec(memory_space=pltpu.SMEM),
            pl.BlockSpec((N_BIAS_TILES * tb, tb), lambda h: (0, 0)),
        ],
        out_specs=pl.BlockSpec((None, N_BIAS_TILES * tb, tb), lambda h: (h, 0, 0)),
        out_shape=jax.ShapeDtypeStruct((N_HEADS, N_BIAS_TILES * tb, tb), F32),
        compiler_params=_params("parallel"),
        name="t5_bias_tiles",
    )(rel_bias, ids)
    return out.reshape(N_HEADS, N_BIAS_TILES, tb, tb)


def _lane_wide(x, width):
    return jnp.concatenate([x] * (width // LANES), axis=1)


def _attn_kernel(lam_init_ref, lam_ref, g_ref, q_ref, k_ref, v_ref, band_ref, o_ref, s_ref, mx_ref,
                 *, n_items, n_keys):
    tq = ATTN_TILE
    sub = n_keys // ATTN_TILE
    maps = 2
    assert n_items >= 2
    group = pl.program_id(2)
    nt = (((1,), (1,)), ((), ()))

    lam_init = lam_init_ref[0]
    lam = lam_ref[...]
    lam_full = (jnp.exp(jnp.sum(lam[0:1] * lam[1:2], axis=-1, keepdims=True))
                - jnp.exp(jnp.sum(lam[2:3] * lam[3:4], axis=-1, keepdims=True)) + lam_init)
    gain = g_ref[...] * (1.0 - lam_init)

    def rows_of(i):
        return pl.ds(i * tq, tq) if isinstance(i, int) else pl.ds(pl.multiple_of(i * tq, tq), tq)

    def qk_cols(m):
        return slice(m * HEAD_DIM, (m + 1) * HEAD_DIM)

    def half_max(x):
        return jnp.maximum(x[:, :LANES], x[:, LANES:])

    def half_sum(x):
        return x[:, :LANES] + x[:, LANES:]

    def run(a_item, b_item):
        if a_item is not None:
            a_row = group * n_items + a_item
            qa = q_ref[rows_of(a_item), :]
            zero = jnp.zeros((tq, HEAD_DIM), qa.dtype)
            a_q2 = jnp.concatenate([jnp.concatenate([qa[:, :HEAD_DIM], zero], axis=1),
                                    jnp.concatenate([zero, qa[:, HEAD_DIM:]], axis=1)], axis=0)
            a_max = [None] * maps
        if b_item is not None:
            b_m_wide = [_lane_wide(mx_ref[m], ATTN_TILE) for m in range(maps)]
            b_sum = [None] * maps
            b_acc2 = None

        for u in range(sub):
            cols = slice(u * ATTN_TILE, (u + 1) * ATTN_TILE)
            if b_item is not None:
                ps = []
                for m in range(maps):
                    p = jnp.exp2(s_ref[m, :, cols] - b_m_wide[m])
                    b_sum[m] = half_sum(p) if b_sum[m] is None else b_sum[m] + half_sum(p)
                    ps.append(p.astype(BF16))
                pv = jnp.dot(jnp.concatenate(ps, axis=0), v_ref[cols, :], preferred_element_type=F32)
                b_acc2 = pv if b_acc2 is None else pv + b_acc2
            if a_item is not None:
                tile = band_ref[jnp.clip(u - a_row, -2, 2) + 2]
                s2 = lax.dot_general(a_q2, k_ref[cols, :], nt, preferred_element_type=F32)
                for m in range(maps):
                    s = s2[m * tq:(m + 1) * tq] + tile
                    s_ref[m, :, cols] = s
                    a_max[m] = half_max(s) if a_max[m] is None else jnp.maximum(a_max[m], half_max(s))

        if a_item is not None:
            for m in range(maps):
                mx_ref[m] = jnp.broadcast_to(jnp.max(a_max[m], axis=-1, keepdims=True), (tq, LANES))
        if b_item is not None:
            o1 = b_acc2[:tq] / jnp.sum(b_sum[0], axis=-1, keepdims=True)
            o2 = b_acc2[tq:] / jnp.sum(b_sum[1], axis=-1, keepdims=True)
            of = o1 - lam_full * o2
            of = of * lax.rsqrt(jnp.mean(of * of, axis=-1, keepdims=True) + LN_EPS)
            o_ref[rows_of(b_item), :] = (of * gain).astype(o_ref.dtype)

    run(0, None)

    def body(t, carry):
        run(t + 1, t)
        return carry

    lax.fori_loop(0, n_items - 1, body, 0)
    run(None, n_items - 1)


def _diff_attention(qkvf, lam_l, subln_g_l, band, batch, seq, lam_init):
    tq = _tile(seq, ATTN_ROWS_PER_STEP)
    assert tq % ATTN_TILE == 0 and seq % tq == 0 and seq % ATTN_TILE == 0
    qkv3 = qkvf.reshape(batch, seq, qkvf.shape[-1])
    kern = functools.partial(_attn_kernel, n_items=tq // ATTN_TILE, n_keys=seq)
    out = pl.pallas_call(
        kern,
        grid=(batch, N_HEADS, seq // tq),
        in_specs=[
            pl.BlockSpec(memory_space=pltpu.SMEM),
            pl.BlockSpec((4, HEAD_DIM), lambda b, h, i: (0, 0)),
            pl.BlockSpec((1, V_DIM), lambda b, h, i: (0, 0)),
            pl.BlockSpec((None, tq, V_DIM), lambda b, h, i: (b, i, h)),
            pl.BlockSpec((None, seq, V_DIM), lambda b, h, i: (b, 0, N_HEADS + h)),
            pl.BlockSpec((None, seq, V_DIM), lambda b, h, i: (b, 0, 2 * N_HEADS + h)),
            pl.BlockSpec((None, N_BIAS_TILES, ATTN_TILE, ATTN_TILE), lambda b, h, i: (h, 0, 0, 0)),
        ],
        out_specs=pl.BlockSpec((None, tq, V_DIM), lambda b, h, i: (b, i, h)),
        out_shape=jax.ShapeDtypeStruct((batch, seq, N_HEADS * V_DIM), BF16),
        scratch_shapes=[
            pltpu.VMEM((2, ATTN_TILE, seq), F32),
            pltpu.VMEM((2, ATTN_TILE, LANES), F32),
        ],
        compiler_params=_params("parallel", "parallel", "arbitrary"),
        name="diff_attention",
    )(jnp.full((1,), lam_init, F32), lam_l, subln_g_l.reshape(1, V_DIM), qkv3, qkv3, qkv3, band)
    return out.reshape(batch * seq, N_HEADS * V_DIM)


FNET_S2 = 128
ROW_TILE = 8
FNET_STAGE1_OUT_BYTES = 8 * 1024 * 1024


def _angles(j, k, n):
    return ((j * k) % n).astype(F32) * (2.0 * math.pi / n)


def _chan_dft_table():
    n = FGROUP_DIM
    ang = _angles(lax.broadcasted_iota(jnp.int32, (n, n), 0), lax.broadcasted_iota(jnp.int32, (n, n), 1), n)
    return (jnp.concatenate([jnp.cos(ang), jnp.sin(ang)], axis=1) * n ** -0.5).astype(BF16)


def _seq_dft_tables(seq):
    s2 = min(FNET_S2, seq)
    s1 = seq // s2
    r = min(ROW_TILE, s1)
    eye1 = jnp.eye(ROW_TILE, dtype=F32)
    eye2 = jnp.eye(r, dtype=F32)
    a1 = _angles(lax.broadcasted_iota(jnp.int32, (s1, s1), 0), lax.broadcasted_iota(jnp.int32, (s1, s1), 1), s1)
    c1, sn1 = jnp.cos(a1), jnp.sin(a1)
    m1 = jnp.concatenate([jnp.concatenate([c1, -sn1], axis=1), jnp.concatenate([-sn1, -c1], axis=1)], axis=0)
    m1 = m1 * s1 ** -0.5
    k1m = (m1[:, None, :, None] * eye1[None, :, None, :]).reshape(2 * s1 * ROW_TILE, 2 * s1 * ROW_TILE)

    k1 = lax.broadcasted_iota(jnp.int32, (s1, s2, s2), 0)
    k2 = lax.broadcasted_iota(jnp.int32, (s1, s2, s2), 1)
    n2 = lax.broadcasted_iota(jnp.int32, (s1, s2, s2), 2)
    a2 = _angles(k1 + s1 * k2, n2, seq)
    g = jnp.stack([jnp.cos(a2), jnp.sin(a2)], axis=0) * s2 ** -0.5
    g = g.reshape(2, s1 // r, r, s2, s2)
    k2m = jnp.einsum("pgjkn,jq->gkjpqn", g, eye2).reshape(s1 // r, s2 * r, 2 * r * s2)
    return k1m.astype(BF16), k2m.astype(BF16)


def _chan_stage1_kernel(cs_ref, m_ref, f_ref, o_ref, *, tiles):
    s1, rows, fw = f_ref.shape
    f = f_ref[...].reshape(s1 * rows, fw)
    cs = cs_ref[...]
    zc, zs = [], []
    for g in range(N_FGROUPS):
        z = jnp.dot(f[:, g * FGROUP_DIM:(g + 1) * FGROUP_DIM], cs, preferred_element_type=F32)
        zc.append(z[:, :FGROUP_DIM])
        zs.append(z[:, FGROUP_DIM:])
    z = jnp.stack([jnp.concatenate(zc, axis=1), jnp.concatenate(zs, axis=1)], axis=0)
    z = z.reshape(2, s1, tiles, ROW_TILE, fw)
    for t in range(0, tiles, 2):
        ys = []
        for tt in (t, t + 1):
            x = z[:, :, tt].reshape(2 * s1 * ROW_TILE, fw).astype(BF16)
            y = jnp.dot(m_ref[...], x, preferred_element_type=F32)
            ys.append(y.reshape(2, s1, ROW_TILE, fw))
        o_ref[:, :, t * ROW_TILE:(t + 2) * ROW_TILE, :] = jnp.concatenate(ys, axis=2).astype(o_ref.dtype)


def _seq_stage2_kernel(m_ref, x_ref, o_ref):
    width = x_ref.shape[-1]
    x = x_ref[...].reshape(-1, width)
    o_ref[...] = jnp.dot(m_ref[...], x, preferred_element_type=F32).reshape(o_ref.shape)


def _fourier_mix(qkvf, f_col_block, chan_cs, seq_tabs, batch, seq):
    fw = N_FGROUPS * FGROUP_DIM
    k1m, k2m = seq_tabs
    s2 = min(FNET_S2, seq)
    s1 = seq // s2
    r = min(ROW_TILE, s1)
    assert s2 % ROW_TILE == 0 and s1 % r == 0
    n2_rows = max(2 * ROW_TILE, min(s2, FNET_STAGE1_OUT_BYTES // (2 * s1 * fw * 4)))
    assert s2 % n2_rows == 0 and n2_rows % (2 * ROW_TILE) == 0
    tiles = n2_rows // ROW_TILE
    qkv4 = qkvf.reshape(batch, s1, s2, qkvf.shape[-1])
    a = pl.pallas_call(
        functools.partial(_chan_stage1_kernel, tiles=tiles),
        grid=(batch, s2 // n2_rows),
        in_specs=[
            pl.BlockSpec((FGROUP_DIM, 2 * FGROUP_DIM), lambda b, t: (0, 0)),
            pl.BlockSpec(k1m.shape, lambda b, t: (0, 0)),
            pl.BlockSpec((None, s1, n2_rows, fw), lambda b, t: (b, 0, t, f_col_block)),
        ],
        out_specs=pl.BlockSpec((None, 2, s1, n2_rows, fw), lambda b, t: (b, 0, 0, t, 0)),
        out_shape=jax.ShapeDtypeStruct((batch, 2, s1, s2, fw), BF16),
        compiler_params=_params("parallel", "parallel"),
        name="fnet_chan_dft_stage1",
    )(chan_cs, k1m, qkv4)

    y = pl.pallas_call(
        _seq_stage2_kernel,
        grid=(s1 // r, batch),
        in_specs=[
            pl.BlockSpec((None,) + k2m.shape[1:], lambda g, b: (g, 0, 0)),
            pl.BlockSpec((None, 2, r, s2, fw), lambda g, b: (b, 0, g, 0, 0)),
        ],
        out_specs=pl.BlockSpec((None, s2, r, fw), lambda g, b: (b, 0, g, 0)),
        out_shape=jax.ShapeDtypeStruct((batch, s2, s1, fw), F32),
        compiler_params=_params("parallel", "parallel"),
        name="fnet_seq_dft_stage2",
    )(k2m, a)
    return y.reshape(batch * seq, fw)


def _merge_kernel(o_ref, f_ref, wa_ref, wf_ref, g1_ref, g2_ref, out_ref):
    ya = jnp.dot(o_ref[...], wa_ref[...], preferred_element_type=F32)
    yf = jnp.dot(f_ref[...].astype(BF16), wf_ref[...], preferred_element_type=F32)
    out_ref[...] = (g1_ref[...].astype(F32) * ya + g2_ref[...].astype(F32) * yf).astype(out_ref.dtype)


def _gated_merge(o, fm, gates, w_attn, w_fnet, layer):
    t, ka = o.shape
    kf = fm.shape[1]
    d = w_attn.shape[-1]
    bm = _tile(t, 1024)
    bn = _tile(d, 1024)
    nb = d // bn
    return pl.pallas_call(
        _merge_kernel,
        grid=(nb, t // bm),
        in_specs=[
            pl.BlockSpec((bm, ka), lambda j, i: (i, 0)),
            pl.BlockSpec((bm, kf), lambda j, i: (i, 0)),
            pl.BlockSpec((None, ka, bn), lambda j, i: (layer, 0, j)),
            pl.BlockSpec((None, kf, bn), lambda j, i: (layer, 0, j)),
            pl.BlockSpec((bm, bn), lambda j, i: (i, j)),
            pl.BlockSpec((bm, bn), lambda j, i: (i, nb + j)),
        ],
        out_specs=pl.BlockSpec((bm, bn), lambda j, i: (i, j)),
        out_shape=jax.ShapeDtypeStruct((t, d), BF16),
        compiler_params=_params("parallel", "parallel"),
        name="gated_merge",
    )(o, fm, w_attn, w_fnet, gates, gates)


def _proj_res_ln_kernel(a_ref, w_ref, x_ref, g_ref, b_ref, of_ref, ob_ref, acc_ref, *, alpha):
    k = pl.program_id(1)

    @pl.when(k == 0)
    def _():
        acc_ref[...] = jnp.zeros_like(acc_ref)

    acc_ref[...] += jnp.dot(a_ref[...], w_ref[...], preferred_element_type=F32)

    @pl.when(k == pl.num_programs(1) - 1)
    def _():
        y = _layernorm_rows(alpha * x_ref[...] + acc_ref[...], g_ref[...], b_ref[...])
        of_ref[...] = y
        ob_ref[...] = y.astype(BF16)


def _proj_residual_layernorm(a, w, layer, xf, g, b, alpha, bk_pref):
    t, kdim = a.shape
    d = w.shape[-1]
    bm = _tile(t, 512)
    bk = _tile(kdim, bk_pref)
    row = pl.BlockSpec((bm, d), lambda i, k: (i, 0))
    vec = pl.BlockSpec((None, 1, d), lambda i, k: (layer, 0, 0))
    return pl.pallas_call(
        functools.partial(_proj_res_ln_kernel, alpha=alpha),
        grid=(t // bm, kdim // bk),
        in_specs=[
            pl.BlockSpec((bm, bk), lambda i, k: (i, k)),
            pl.BlockSpec((None, bk, d), lambda i, k: (layer, k, 0)),
            row, vec, vec,
        ],
        out_specs=[row, row],
        out_shape=[jax.ShapeDtypeStruct((t, d), F32), jax.ShapeDtypeStruct((t, d), BF16)],
        scratch_shapes=[pltpu.VMEM((bm, d), F32)],
        compiler_params=_params("parallel", "arbitrary"),
        name="proj_residual_layernorm",
    )(a, w, xf, g, b)


def _swiglu_kernel(x_ref, wg_ref, wu_ref, o_ref):
    x = x_ref[...]
    gate = jnp.dot(x, wg_ref[...], preferred_element_type=F32)
    up = jnp.dot(x, wu_ref[...], preferred_element_type=F32)
    o_ref[...] = (gate * jax.nn.sigmoid(gate) * up).astype(o_ref.dtype)


def _swiglu_up(xb, w_gu, layer, d_ff):
    t, d = xb.shape
    bm = _tile(t, 1024)
    bn = _tile(d_ff, 512)
    nb = d_ff // bn
    return pl.pallas_call(
        _swiglu_kernel,
        grid=(nb, t // bm),
        in_specs=[
            pl.BlockSpec((bm, d), lambda j, i: (i, 0)),
            pl.BlockSpec((None, d, bn), lambda j, i: (layer, 0, j)),
            pl.BlockSpec((None, d, bn), lambda j, i: (layer, 0, nb + j)),
        ],
        out_specs=pl.BlockSpec((bm, bn), lambda j, i: (i, j)),
        out_shape=jax.ShapeDtypeStruct((t, d_ff), BF16),
        compiler_params=_params("parallel", "parallel"),
        name="swiglu_up",
    )(xb, w_gu, w_gu)


def _trunk(x, band, chan_cs, seq_tabs, p):
    batch, seq, d = x.shape
    depth = p["w_in"].shape[0]
    alpha = (2.0 * depth) ** 0.25
    qk_width = N_HEADS * 2 * HEAD_DIM
    attn_width = N_HEADS * V_DIM
    fnet_width = N_FGROUPS * FGROUP_DIM
    n_main = 2 * qk_width + attn_width + fnet_width
    assert (2 * qk_width + attn_width) % fnet_width == 0
    f_col_block = (2 * qk_width + attn_width) // fnet_width
    d_ff = p["w_down"].shape[1]

    col_scale = jnp.concatenate(
        [jnp.full((qk_width,), HEAD_DIM ** -0.5 * LOG2_E, F32), jnp.ones((n_main - qk_width,), F32)]).reshape(1, n_main)

    xf, xb = _input_layernorm(x.reshape(batch * seq, d), p["ln_in_g"], p["ln_in_b"])
    for l in range(depth):
        lam_init = 0.8 - 0.6 * math.exp(-0.3 * l)
        qkvf, gates = _in_projection(xb, p["w_in"], l, col_scale, p["b_gate"], n_main)
        o = _diff_attention(qkvf, p["lam"][l], p["subln_g"][l], band, batch, seq, lam_init)
        fm = _fourier_mix(qkvf, f_col_block, chan_cs, seq_tabs, batch, seq)
        merged = _gated_merge(o, fm, gates, p["w_br_attn"], p["w_br_fnet"], l)
        xf, xb = _proj_residual_layernorm(merged, p["w_out"], l, xf, p["ln1_g"], p["ln1_b"], alpha, 2048)
        act = _swiglu_up(xb, p["w_gu"], l, d_ff)
        xf, xb = _proj_residual_layernorm(act, p["w_down"], l, xf, p["ln2_g"], p["ln2_b"], alpha, 2816)
    return xf.reshape(batch, seq, d)


def kernel(x_prompt, x_sample, rel_bias, ln_in_g, ln_in_b, w_in, b_gate, lam, subln_g, w_br_attn, w_br_fnet,
           w_out, ln1_g, ln1_b, w_gu, w_down, ln2_g, ln2_b):
    depth, d = ln1_g.shape
    p = {
        "ln_in_g": ln_in_g, "ln_in_b": ln_in_b,
        "w_in": w_in.astype(BF16), "b_gate": b_gate.reshape(depth, 1, -1),
        "lam": lam, "subln_g": subln_g,
        "w_br_attn": w_br_attn.astype(BF16), "w_br_fnet": w_br_fnet.astype(BF16),
        "w_out": w_out.astype(BF16), "w_gu": w_gu.astype(BF16), "w_down": w_down.astype(BF16),
        "ln1_g": ln1_g.reshape(depth, 1, d), "ln1_b": ln1_b.reshape(depth, 1, d),
        "ln2_g": ln2_g.reshape(depth, 1, d), "ln2_b": ln2_b.reshape(depth, 1, d),
    }
    band = _bias_tiles(rel_bias)
    chan_cs = _chan_dft_table()
    return tuple(_trunk(x, band, chan_cs, _seq_dft_tables(x.shape[1]), p) for x in (x_prompt, x_sample))
```

```python
import functools
import math

import jax
import jax.numpy as jnp
from jax import lax
from jax.experimental import pallas as pl
from jax.experimental.pallas import tpu as pltpu

N_HEADS = 8
HEAD_DIM = 128
V_DIM = 2 * HEAD_DIM
N_FGROUPS = 8
FGROUP_DIM = 128
N_BUCKETS = 32
MAX_DISTANCE = 128
LN_EPS = 1e-5
LOG2_E = math.log2(math.e)

LANES = 128
ATTN_TILE = 256
ATTN_STACK_MIN_TILES = 16
ATTN_ROWS_PER_STEP = 2048
VMEM_LIMIT_BYTES = 56 * 1024 * 1024

F32 = jnp.float32
BF16 = jnp.bfloat16


def _tile(n, pref):
    if n <= pref:
        return n
    t = (pref // LANES) * LANES
    while t >= LANES:
        if n % t == 0:
            return t
        t -= LANES
    raise ValueError(f"no lane-aligned tile for {n}")


def _params(*sem):
    return pltpu.CompilerParams(dimension_semantics=sem, vmem_limit_bytes=VMEM_LIMIT_BYTES)


def _layernorm_rows(y, g, b):
    mu = jnp.mean(y, axis=-1, keepdims=True)
    yc = y - mu
    var = jnp.mean(yc * yc, axis=-1, keepdims=True)
    return yc * lax.rsqrt(var + LN_EPS) * g + b


def _ln_kernel(x_ref, g_ref, b_ref, of_ref, ob_ref):
    y = _layernorm_rows(x_ref[...], g_ref[...], b_ref[...])
    of_ref[...] = y
    ob_ref[...] = y.astype(BF16)


def _input_layernorm(x, g, b):
    t, d = x.shape
    bm = _tile(t, 1024)
    row = pl.BlockSpec((bm, d), lambda i: (i, 0))
    vec = pl.BlockSpec((1, d), lambda i: (0, 0))
    return pl.pallas_call(
        _ln_kernel,
        grid=(t // bm,),
        in_specs=[row, vec, vec],
        out_specs=[row, row],
        out_shape=[jax.ShapeDtypeStruct((t, d), F32), jax.ShapeDtypeStruct((t, d), BF16)],
        compiler_params=_params("parallel"),
        name="input_layernorm",
    )(x, g.reshape(1, d), b.reshape(1, d))


def _proj_kernel(x_ref, w_ref, s_ref, o_ref):
    acc = jnp.dot(x_ref[...], w_ref[...], preferred_element_type=F32)
    o_ref[...] = (acc * s_ref[...]).astype(o_ref.dtype)


def _gate_kernel(x_ref, w_ref, b_ref, o_ref):
    acc = jnp.dot(x_ref[...], w_ref[...], preferred_element_type=F32)
    o_ref[...] = jax.nn.sigmoid(acc + b_ref[...]).astype(o_ref.dtype)


def _in_projection(xb, w_in, layer, col_scale, b_gate, n_main):
    t, d = xb.shape
    n_gate = w_in.shape[-1] - n_main
    bm = _tile(t, 1024)

    bn = _tile(n_main, 1792)
    qkvf = pl.pallas_call(
        _proj_kernel,
        grid=(n_main // bn, t // bm),
        in_specs=[
            pl.BlockSpec((bm, d), lambda j, i: (i, 0)),
            pl.BlockSpec((None, d, bn), lambda j, i: (layer, 0, j)),
            pl.BlockSpec((1, bn), lambda j, i: (0, j)),
        ],
        out_specs=pl.BlockSpec((bm, bn), lambda j, i: (i, j)),
        out_shape=jax.ShapeDtypeStruct((t, n_main), BF16),
        compiler_params=_params("parallel", "parallel"),
        name="in_proj_qkvf",
    )(xb, w_in, col_scale)

    bg = _tile(n_gate, 1024)
    assert n_main % bg == 0
    off = n_main // bg
    gates = pl.pallas_call(
        _gate_kernel,
        grid=(n_gate // bg, t // bm),
        in_specs=[
            pl.BlockSpec((bm, d), lambda j, i: (i, 0)),
            pl.BlockSpec((None, d, bg), lambda j, i: (layer, 0, off + j)),
            pl.BlockSpec((None, 1, bg), lambda j, i: (layer, 0, j)),
        ],
        out_specs=pl.BlockSpec((bm, bg), lambda j, i: (i, j)),
        out_shape=jax.ShapeDtypeStruct((t, n_gate), BF16),
        compiler_params=_params("parallel", "parallel"),
        name="in_proj_gates",
    )(xb, w_in, b_gate)
    return qkvf, gates


def _t5_bucket(rel):
    nb = N_BUCKETS // 2
    ret = jnp.where(rel > 0, nb, 0)
    n = jnp.abs(rel)
    max_exact = nb // 2
    nf = jnp.maximum(n, 1).astype(F32)
    large = max_exact + (jnp.log(nf / max_exact) / math.log(MAX_DISTANCE / max_exact) * (nb - max_exact)).astype(jnp.int32)
    large = jnp.minimum(large, nb - 1)
    return ret + jnp.where(n < max_exact, n, large)


N_BIAS_TILES = 5


def _bias_tile_kernel(tab_ref, ids_ref, o_ref):
    h = pl.program_id(0)
    ids = ids_ref[...]
    acc = jnp.zeros(ids.shape, F32)
    for b in range(N_BUCKETS):
        acc = jnp.where(ids == b, tab_ref[b, h], acc)
    o_ref[...] = acc * LOG2_E


def _bias_tiles(rel_bias):
    tb = ATTN_TILE
    assert tb >= MAX_DISTANCE
    i = lax.broadcasted_iota(jnp.int32, (N_BIAS_TILES, tb, tb), 1)
    j = lax.broadcasted_iota(jnp.int32, (N_BIAS_TILES, tb, tb), 2)
    d = lax.broadcasted_iota(jnp.int32, (N_BIAS_TILES, tb, tb), 0) - 2
    ids = _t5_bucket(d * tb + j - i).reshape(N_BIAS_TILES * tb, tb)
    out = pl.pallas_call(
        _bias_tile_kernel,
        grid=(N_HEADS,),
        in_specs=[
            pl.BlockSpec(memory_space=pltpu.SMEM),
            pl.BlockSpec((N_BIAS_TILES * tb, tb), lambda h: (0, 0)),
        ],
        out_specs=pl.BlockSpec((None, N_BIAS_TILES * tb, tb), lambda h: (h, 0, 0)),
        out_shape=jax.ShapeDtypeStruct((N_HEADS, N_BIAS_TILES * tb, tb), F32),
        compiler_params=_params("parallel"),
        name="t5_bias_tiles",
    )(rel_bias, ids)
    return out.reshape(N_HEADS, N_BIAS_TILES, tb, tb)


def _lane_wide(x, width):
    return jnp.concatenate([x] * (width // LANES), axis=1)


def _attn_kernel(lam_init_ref, lam_ref, g_ref, q_ref, k_ref, v_ref, band_ref, o_ref, s_ref, mx_ref,
                 *, n_items, n_keys):
    tq = ATTN_TILE
    sub = n_keys // ATTN_TILE
    maps = 2
    stacked = sub >= ATTN_STACK_MIN_TILES
    assert n_items >= 2
    group = pl.program_id(2)
    nt = (((1,), (1,)), ((), ()))

    lam_init = lam_init_ref[0]
    lam = lam_ref[...]
    lam_full = (jnp.exp(jnp.sum(lam[0:1] * lam[1:2], axis=-1, keepdims=True))
                - jnp.exp(jnp.sum(lam[2:3] * lam[3:4], axis=-1, keepdims=True)) + lam_init)
    gain = g_ref[...] * (1.0 - lam_init)

    def rows_of(i):
        return pl.ds(i * tq, tq) if isinstance(i, int) else pl.ds(pl.multiple_of(i * tq, tq), tq)

    def qk_cols(m):
        return slice(m * HEAD_DIM, (m + 1) * HEAD_DIM)

    def half_max(x):
        return jnp.maximum(x[:, :LANES], x[:, LANES:])

    def half_sum(x):
        return x[:, :LANES] + x[:, LANES:]

    def run(a_item, b_item):
        if a_item is not None:
            a_row = group * n_items + a_item
            qa = q_ref[rows_of(a_item), :]
            if stacked:
                zero = jnp.zeros((tq, HEAD_DIM), qa.dtype)
                a_q2 = jnp.concatenate([jnp.concatenate([qa[:, :HEAD_DIM], zero], axis=1),
                                        jnp.concatenate([zero, qa[:, HEAD_DIM:]], axis=1)], axis=0)
            else:
                a_q = [qa[:, qk_cols(m)] for m in range(maps)]
            a_max = [None] * maps
        if b_item is not None:
            b_m_wide = [_lane_wide(mx_ref[m], ATTN_TILE) for m in range(maps)]
            b_sum = [None] * maps
            b_acc = [None] * (1 if stacked else maps)

        for u in range(sub):
            cols = slice(u * ATTN_TILE, (u + 1) * ATTN_TILE)
            if b_item is not None:
                ps = []
                for m in range(maps):
                    p = jnp.exp2(s_ref[m, :, cols] - b_m_wide[m])
                    b_sum[m] = half_sum(p) if b_sum[m] is None else b_sum[m] + half_sum(p)
                    ps.append(p.astype(BF16))
                vu = v_ref[cols, :]
                lhs = [jnp.concatenate(ps, axis=0)] if stacked else ps
                for i, a in enumerate(lhs):
                    pv = jnp.dot(a, vu, preferred_element_type=F32)
                    b_acc[i] = pv if b_acc[i] is None else pv + b_acc[i]
            if a_item is not None:
                tile = band_ref[jnp.clip(u - a_row, -2, 2) + 2]
                if stacked:
                    s2 = lax.dot_general(a_q2, k_ref[cols, :], nt, preferred_element_type=F32)
                for m in range(maps):
                    if stacked:
                        s = s2[m * tq:(m + 1) * tq] + tile
                    else:
                        s = lax.dot_general(a_q[m], k_ref[cols, qk_cols(m)], nt, preferred_element_type=F32) + tile
                    s_ref[m, :, cols] = s
                    a_max[m] = half_max(s) if a_max[m] is None else jnp.maximum(a_max[m], half_max(s))

        if a_item is not None:
            for m in range(maps):
                mx_ref[m] = jnp.broadcast_to(jnp.max(a_max[m], axis=-1, keepdims=True), (tq, LANES))
        if b_item is not None:
            pv1, pv2 = (b_acc[0][:tq], b_acc[0][tq:]) if stacked else b_acc
            o1 = pv1 / jnp.sum(b_sum[0], axis=-1, keepdims=True)
            o2 = pv2 / jnp.sum(b_sum[1], axis=-1, keepdims=True)
            of = o1 - lam_full * o2
            of = of * lax.rsqrt(jnp.mean(of * of, axis=-1, keepdims=True) + LN_EPS)
            o_ref[rows_of(b_item), :] = (of * gain).astype(o_ref.dtype)

    run(0, None)

    def body(t, carry):
        run(t + 1, t)
        return carry

    lax.fori_loop(0, n_items - 1, body, 0)
    run(None, n_items - 1)


def _diff_attention(qkvf, lam_l, subln_g_l, band, batch, seq, lam_init):
    tq = _tile(seq, ATTN_ROWS_PER_STEP)
    assert tq % ATTN_TILE == 0 and seq % tq == 0 and seq % ATTN_TILE == 0
    qkv3 = qkvf.reshape(batch, seq, qkvf.shape[-1])
    kern = functools.partial(_attn_kernel, n_items=tq // ATTN_TILE, n_keys=seq)
    out = pl.pallas_call(
        kern,
        grid=(batch, N_HEADS, seq // tq),
        in_specs=[
            pl.BlockSpec(memory_space=pltpu.SMEM),
            pl.BlockSpec((4, HEAD_DIM), lambda b, h, i: (0, 0)),
            pl.BlockSpec((1, V_DIM), lambda b, h, i: (0, 0)),
            pl.BlockSpec((None, tq, V_DIM), lambda b, h, i: (b, i, h)),
            pl.BlockSpec((None, seq, V_DIM), lambda b, h, i: (b, 0, N_HEADS + h)),
            pl.BlockSpec((None, seq, V_DIM), lambda b, h, i: (b, 0, 2 * N_HEADS + h)),
            pl.BlockSpec((None, N_BIAS_TILES, ATTN_TILE, ATTN_TILE), lambda b, h, i: (h, 0, 0, 0)),
        ],
        out_specs=pl.BlockSpec((None, tq, V_DIM), lambda b, h, i: (b, i, h)),
        out_shape=jax.ShapeDtypeStruct((batch, seq, N_HEADS * V_DIM), BF16),
        scratch_shapes=[
            pltpu.VMEM((2, ATTN_TILE, seq), F32),
            pltpu.VMEM((2, ATTN_TILE, LANES), F32),
        ],
        compiler_params=_params("parallel", "parallel", "arbitrary"),
        name="diff_attention",
    )(jnp.full((1,), lam_init, F32), lam_l, subln_g_l.reshape(1, V_DIM), qkv3, qkv3, qkv3, band)
    return out.reshape(batch * seq, N_HEADS * V_DIM)


FNET_S2 = 128
ROW_TILE = 8
FNET_STAGE1_OUT_BYTES = 8 * 1024 * 1024


def _angles(j, k, n):
    return ((j * k) % n).astype(F32) * (2.0 * math.pi / n)


def _chan_dft_table():
    n = FGROUP_DIM
    ang = _angles(lax.broadcasted_iota(jnp.int32, (n, n), 0), lax.broadcasted_iota(jnp.int32, (n, n), 1), n)
    return (jnp.concatenate([jnp.cos(ang), jnp.sin(ang)], axis=1) * n ** -0.5).astype(BF16)


def _seq_dft_tables(seq):
    s2 = min(FNET_S2, seq)
    s1 = seq // s2
    r = min(ROW_TILE, s1)
    eye1 = jnp.eye(ROW_TILE, dtype=F32)
    eye2 = jnp.eye(r, dtype=F32)
    a1 = _angles(lax.broadcasted_iota(jnp.int32, (s1, s1), 0), lax.broadcasted_iota(jnp.int32, (s1, s1), 1), s1)
    c1, sn1 = jnp.cos(a1), jnp.sin(a1)
    m1 = jnp.concatenate([jnp.concatenate([c1, -sn1], axis=1), jnp.concatenate([-sn1, -c1], axis=1)], axis=0)
    m1 = m1 * s1 ** -0.5
    k1m = (m1[:, None, :, None] * eye1[None, :, None, :]).reshape(2 * s1 * ROW_TILE, 2 * s1 * ROW_TILE)

    k1 = lax.broadcasted_iota(jnp.int32, (s1, s2, s2), 0)
    k2 = lax.broadcasted_iota(jnp.int32, (s1, s2, s2), 1)
    n2 = lax.broadcasted_iota(jnp.int32, (s1, s2, s2), 2)
    a2 = _angles(k1 + s1 * k2, n2, seq)
    g = jnp.stack([jnp.cos(a2), jnp.sin(a2)], axis=0) * s2 ** -0.5
    g = g.reshape(2, s1 // r, r, s2, s2)
    k2m = jnp.einsum("pgjkn,jq->gkjpqn", g, eye2).reshape(s1 // r, s2 * r, 2 * r * s2)
    return k1m.astype(BF16), k2m.astype(BF16)


def _chan_stage1_kernel(cs_ref, m_ref, f_ref, o_ref, *, tiles):
    s1, rows, fw = f_ref.shape
    f = f_ref[...].reshape(s1 * rows, fw)
    cs = cs_ref[...]
    zc, zs = [], []
    for g in range(N_FGROUPS):
        z = jnp.dot(f[:, g * FGROUP_DIM:(g + 1) * FGROUP_DIM], cs, preferred_element_type=F32)
        zc.append(z[:, :FGROUP_DIM])
        zs.append(z[:, FGROUP_DIM:])
    z = jnp.stack([jnp.concatenate(zc, axis=1), jnp.concatenate(zs, axis=1)], axis=0)
    z = z.reshape(2, s1, tiles, ROW_TILE, fw)
    for t in range(0, tiles, 2):
        ys = []
        for tt in (t, t + 1):
            x = z[:, :, tt].reshape(2 * s1 * ROW_TILE, fw).astype(BF16)
            y = jnp.dot(m_ref[...], x, preferred_element_type=F32)
            ys.append(y.reshape(2, s1, ROW_TILE, fw))
        o_ref[:, :, t * ROW_TILE:(t + 2) * ROW_TILE, :] = jnp.concatenate(ys, axis=2).astype(o_ref.dtype)


def _seq_stage2_kernel(m_ref, x_ref, o_ref):
    width = x_ref.shape[-1]
    x = x_ref[...].reshape(-1, width)
    o_ref[...] = jnp.dot(m_ref[...], x, preferred_element_type=F32).reshape(o_ref.shape)


def _fourier_mix(qkvf, f_col_block, chan_cs, seq_tabs, batch, seq):
    fw = N_FGROUPS * FGROUP_DIM
    k1m, k2m = seq_tabs
    s2 = min(FNET_S2, seq)
    s1 = seq // s2
    r = min(ROW_TILE, s1)
    assert s2 % ROW_TILE == 0 and s1 % r == 0
    n2_rows = max(2 * ROW_TILE, min(s2, FNET_STAGE1_OUT_BYTES // (2 * s1 * fw * 4)))
    assert s2 % n2_rows == 0 and n2_rows % (2 * ROW_TILE) == 0
    tiles = n2_rows // ROW_TILE
    qkv4 = qkvf.reshape(batch, s1, s2, qkvf.shape[-1])
    a = pl.pallas_call(
        functools.partial(_chan_stage1_kernel, tiles=tiles),
        grid=(batch, s2 // n2_rows),
        in_specs=[
            pl.BlockSpec((FGROUP_DIM, 2 * FGROUP_DIM), lambda b, t: (0, 0)),
            pl.BlockSpec(k1m.shape, lambda b, t: (0, 0)),
            pl.BlockSpec((None, s1, n2_rows, fw), lambda b, t: (b, 0, t, f_col_block)),
        ],
        out_specs=pl.BlockSpec((None, 2, s1, n2_rows, fw), lambda b, t: (b, 0, 0, t, 0)),
        out_shape=jax.ShapeDtypeStruct((batch, 2, s1, s2, fw), BF16),
        compiler_params=_params("parallel", "parallel"),
        name="fnet_chan_dft_stage1",
    )(chan_cs, k1m, qkv4)

    y = pl.pallas_call(
        _seq_stage2_kernel,
        grid=(s1 // r, batch),
        in_specs=[
            pl.BlockSpec((None,) + k2m.shape[1:], lambda g, b: (g, 0, 0)),
            pl.BlockSpec((None, 2, r, s2, fw), lambda g, b: (b, 0, g, 0, 0)),
        ],
        out_specs=pl.BlockSpec((None, s2, r, fw), lambda g, b: (b, 0, g, 0)),
        out_shape=jax.ShapeDtypeStruct((batch, s2, s1, fw), F32),
        compiler_params=_params("parallel", "parallel"),
        name="fnet_seq_dft_stage2",
    )(k2m, a)
    return y.reshape(batch * seq, fw)


def _merge_kernel(o_ref, f_ref, wa_ref, wf_ref, g1_ref, g2_ref, out_ref):
    ya = jnp.dot(o_ref[...], wa_ref[...], preferred_element_type=F32)
    yf = jnp.dot(f_ref[...].astype(BF16), wf_ref[...], preferred_element_type=F32)
    out_ref[...] = (g1_ref[...].astype(F32) * ya + g2_ref[...].astype(F32) * yf).astype(out_ref.dtype)


def _gated_merge(o, fm, gates, w_attn, w_fnet, layer):
    t, ka = o.shape
    kf = fm.shape[1]
    d = w_attn.shape[-1]
    bm = _tile(t, 1024)
    bn = _tile(d, 1024)
    nb = d // bn
    return pl.pallas_call(
        _merge_kernel,
        grid=(nb, t // bm),
        in_specs=[
            pl.BlockSpec((bm, ka), lambda j, i: (i, 0)),
            pl.BlockSpec((bm, kf), lambda j, i: (i, 0)),
            pl.BlockSpec((None, ka, bn), lambda j, i: (layer, 0, j)),
            pl.BlockSpec((None, kf, bn), lambda j, i: (layer, 0, j)),
            pl.BlockSpec((bm, bn), lambda j, i: (i, j)),
            pl.BlockSpec((bm, bn), lambda j, i: (i, nb + j)),
        ],
        out_specs=pl.BlockSpec((bm, bn), lambda j, i: (i, j)),
        out_shape=jax.ShapeDtypeStruct((t, d), BF16),
        compiler_params=_params("parallel", "parallel"),
        name="gated_merge",
    )(o, fm, w_attn, w_fnet, gates, gates)


def _proj_res_ln_kernel(a_ref, w_ref, x_ref, g_ref, b_ref, of_ref, ob_ref, acc_ref, *, alpha):
    k = pl.program_id(1)

    @pl.when(k == 0)
    def _():
        acc_ref[...] = jnp.zeros_like(acc_ref)

    acc_ref[...] += jnp.dot(a_ref[...], w_ref[...], preferred_element_type=F32)

    @pl.when(k == pl.num_programs(1) - 1)
    def _():
        y = _layernorm_rows(alpha * x_ref[...] + acc_ref[...], g_ref[...], b_ref[...])
        of_ref[...] = y
        ob_ref[...] = y.astype(BF16)


def _proj_residual_layernorm(a, w, layer, xf, g, b, alpha, bk_pref):
    t, kdim = a.shape
    d = w.shape[-1]
    bm = _tile(t, 512)
    bk = _tile(kdim, bk_pref)
    row = pl.BlockSpec((bm, d), lambda i, k: (i, 0))
    vec = pl.BlockSpec((None, 1, d), lambda i, k: (layer, 0, 0))
    return pl.pallas_call(
        functools.partial(_proj_res_ln_kernel, alpha=alpha),
        grid=(t // bm, kdim // bk),
        in_specs=[
            pl.BlockSpec((bm, bk), lambda i, k: (i, k)),
            pl.BlockSpec((None, bk, d), lambda i, k: (layer, k, 0)),
            row, vec, vec,
        ],
        out_specs=[row, row],
        out_shape=[jax.ShapeDtypeStruct((t, d), F32), jax.ShapeDtypeStruct((t, d), BF16)],
        scratch_shapes=[pltpu.VMEM((bm, d), F32)],
        compiler_params=_params("parallel", "arbitrary"),
        name="proj_residual_layernorm",
    )(a, w, xf, g, b)


def _swiglu_kernel(x_ref, wg_ref, wu_ref, o_ref):
    x = x_ref[...]
    gate = jnp.dot(x, wg_ref[...], preferred_element_type=F32)
    up = jnp.dot(x, wu_ref[...], preferred_element_type=F32)
    o_ref[...] = (gate * jax.nn.sigmoid(gate) * up).astype(o_ref.dtype)


def _swiglu_up(xb, w_gu, layer, d_ff):
    t, d = xb.shape
    bm = _tile(t, 1024)
    bn = _tile(d_ff, 512)
    nb = d_ff // bn
    return pl.pallas_call(
        _swiglu_kernel,
        grid=(nb, t // bm),
        in_specs=[
            pl.BlockSpec((bm, d), lambda j, i: (i, 0)),
            pl.BlockSpec((None, d, bn), lambda j, i: (layer, 0, j)),
            pl.BlockSpec((None, d, bn), lambda j, i: (layer, 0, nb + j)),
        ],
        out_specs=pl.BlockSpec((bm, bn), lambda j, i: (i, j)),
        out_shape=jax.ShapeDtypeStruct((t, d_ff), BF16),
        compiler_params=_params("parallel", "parallel"),
        name="swiglu_up",
    )(xb, w_gu, w_gu)


def _trunk(x, band, chan_cs, seq_tabs, p):
    batch, seq, d = x.shape
    depth = p["w_in"].shape[0]
    alpha = (2.0 * depth) ** 0.25
    qk_width = N_HEADS * 2 * HEAD_DIM
    attn_width = N_HEADS * V_DIM
    fnet_width = N_FGROUPS * FGROUP_DIM
    n_main = 2 * qk_width + attn_width + fnet_width
    assert (2 * qk_width + attn_width) % fnet_width == 0
    f_col_block = (2 * qk_width + attn_width) // fnet_width
    d_ff = p["w_down"].shape[1]

    col_scale = jnp.concatenate(
        [jnp.full((qk_width,), HEAD_DIM ** -0.5 * LOG2_E, F32), jnp.ones((n_main - qk_width,), F32)]).reshape(1, n_main)

    xf, xb = _input_layernorm(x.reshape(batch * seq, d), p["ln_in_g"], p["ln_in_b"])
    for l in range(depth):
        lam_init = 0.8 - 0.6 * math.exp(-0.3 * l)
        qkvf, gates = _in_projection(xb, p["w_in"], l, col_scale, p["b_gate"], n_main)
        o = _diff_attention(qkvf, p["lam"][l], p["subln_g"][l], band, batch, seq, lam_init)
        fm = _fourier_mix(qkvf, f_col_block, chan_cs, seq_tabs, batch, seq)
        merged = _gated_merge(o, fm, gates, p["w_br_attn"], p["w_br_fnet"], l)
        xf, xb = _proj_residual_layernorm(merged, p["w_out"], l, xf, p["ln1_g"], p["ln1_b"], alpha, 2048)
        act = _swiglu_up(xb, p["w_gu"], l, d_ff)
        xf, xb = _proj_residual_layernorm(act, p["w_down"], l, xf, p["ln2_g"], p["ln2_b"], alpha, 2816)
    return xf.reshape(batch, seq, d)


def kernel(x_prompt, x_sample, rel_bias, ln_in_g, ln_in_b, w_in, b_gate, lam, subln_g, w_br_attn, w_br_fnet,
           w_out, ln1_g, ln1_b, w_gu, w_down, ln2_g, ln2_b):
    depth, d = ln1_g.shape
    p = {
        "ln_in_g": ln_in_g, "ln_in_b": ln_in_b,
        "w_in": w_in.astype(BF16), "b_gate": b_gate.reshape(depth, 1, -1),
        "lam": lam, "subln_g": subln_g,
        "w_br_attn": w_br_attn.astype(BF16), "w_br_fnet": w_br_fnet.astype(BF16),
        "w_out": w_out.astype(BF16), "w_gu": w_gu.astype(BF16), "w_down": w_down.astype(BF16),
        "ln1_g": ln1_g.reshape(depth, 1, d), "ln1_b": ln1_b.reshape(depth, 1, d),
        "ln2_g": ln2_g.reshape(depth, 1, d), "ln2_b": ln2_b.reshape(depth, 1, d),
    }
    band = _bias_tiles(rel_bias)
    chan_cs = _chan_dft_table()
    return tuple(_trunk(x, band, chan_cs, _seq_dft_tables(x.shape[1]), p) for x in (x_prompt, x_sample))
```
